```python
import jax, jax.numpy as jnp
from jax import lax
import numpy as np

D_MODEL = 1024
BATCH = 8
SEQ = 2048
DEPTH = 1

D_MIX = D_MODEL
D_REC = D_MIX // 2
D_ATT = D_MIX - D_REC
N_LRU_BLOCKS = 8
LRU_BLOCK = D_REC // N_LRU_BLOCKS
LRU_C = 8.0
CONV_WIDTH = 4
HEAD_DIM = 64
N_HEADS = D_ATT // HEAD_DIM
DILATED_PATTERNS = ((128, 1), (512, 4), (2048, 16))
ROPE_THETA = 10000.0
NORM_EPS = 1e-6
NEG_INF = -1e30
D_IN_PROJ = 2 * D_REC + 4 * D_ATT
SPLIT_IDX = (D_REC, 2 * D_REC, 2 * D_REC + D_ATT, 2 * D_REC + 2 * D_ATT, 2 * D_REC + 3 * D_ATT)

kernel_name = 'hymba_rglru_dilated_attn_layer'


def rms_norm(x, g):
    xf = x.astype(jnp.float32)
    y = xf * lax.rsqrt(jnp.mean(xf * xf, axis=-1, keepdims=True) + NORM_EPS)
    return (y * g.astype(jnp.float32)).astype(x.dtype)


def rotary(x, positions):
    half = HEAD_DIM // 2
    inv_freq = ROPE_THETA ** (-jnp.arange(half, dtype=jnp.float32) / half)
    ang = positions.astype(jnp.float32)[..., None] * inv_freq
    cos = jnp.cos(ang)[:, :, None, :]
    sin = jnp.sin(ang)[:, :, None, :]
    xf = x.astype(jnp.float32)
    x1, x2 = xf[..., :half], xf[..., half:]
    out = jnp.concatenate([x1 * cos - x2 * sin, x2 * cos + x1 * sin], axis=-1)
    return out.astype(x.dtype)


def causal_depthwise_conv(x, w, b):
    S = x.shape[1]
    xp = jnp.pad(x, ((0, 0), (CONV_WIDTH - 1, 0), (0, 0)))
    return sum(xp[:, k:k + S, :] * w[k] for k in range(CONV_WIDTH)) + b


def rg_lru(x, w_a, b_a, w_x, b_x, lam):
    B, S, C = x.shape
    xb = x.reshape(B, S, N_LRU_BLOCKS, LRU_BLOCK)
    r = jax.nn.sigmoid(jnp.einsum('bshi,hij->bshj', xb, w_a).reshape(B, S, C) + b_a)
    i = jax.nn.sigmoid(jnp.einsum('bshi,hij->bshj', xb, w_x).reshape(B, S, C) + b_x)
    log_a = -LRU_C * r.astype(jnp.float32) * jax.nn.softplus(-lam.astype(jnp.float32))
    a = jnp.exp(log_a)
    mult = jnp.sqrt(-jnp.expm1(2.0 * log_a))
    u = mult * (i * x).astype(jnp.float32)

    def combine(left, right):
        a_l, b_l = left
        a_r, b_r = right
        return a_l * a_r, a_r * b_l + b_r

    _, h = lax.associative_scan(combine, (a, u), axis=1)
    return h.astype(x.dtype)


def dilated_attention(q, k, v, window, dilation):
    B, H, S, Dh = q.shape
    L = S // dilation
    blk = window // dilation
    n_blk = -(-L // blk)
    pad = n_blk * blk - L

    def to_blocks(t):
        t = t.reshape(B, H, L, dilation, Dh).transpose(0, 1, 3, 2, 4)
        t = jnp.pad(t, ((0, 0), (0, 0), (0, 0), (0, pad), (0, 0)))
        return t.reshape(B, H, dilation, n_blk, blk, Dh)

    def with_prev(t):
        prev = jnp.pad(t, ((0, 0), (0, 0), (0, 0), (1, 0), (0, 0), (0, 0)))[:, :, :, :-1]
        return jnp.concatenate([prev, t], axis=4)

    qb = to_blocks(q)
    kc = with_prev(to_blocks(k))
    vc = with_prev(to_blocks(v))
    scores = jnp.einsum('bhrnqd,bhrnkd->bhrnqk', qb, kc, preferred_element_type=jnp.float32)
    qi = jnp.arange(blk)[:, None]
    ki = jnp.arange(2 * blk)[None, :]
    dist = qi + blk - ki
    blk_idx = jnp.arange(n_blk)[:, None, None]
    valid = ((dist >= 0) & (dist <= blk))[None] & (blk_idx * blk + ki[None] - blk >= 0)
    scores = jnp.where(valid, scores, NEG_INF)
    m = jnp.max(scores, axis=-1, keepdims=True)
    p = jnp.exp(scores - m)
    denom = jnp.sum(p, axis=-1, keepdims=True)
    out = jnp.einsum('bhrnqk,bhrnkd->bhrnqd', p, vc.astype(jnp.float32)) / denom
    lse = (m + jnp.log(denom))[..., 0]

    def from_blocks(t):
        t = t.reshape(B, H, dilation, n_blk * blk, *t.shape[5:])[:, :, :, :L]
        t = jnp.moveaxis(t, 2, 3)
        return t.reshape(B, H, S, *t.shape[4:])

    return from_blocks(out), from_blocks(lse)


def setup_inputs(seed: int = 0) -> dict:
    key = jax.random.key(seed)
    ks = jax.random.split(key, 20)
    D = D_MODEL
    nrm = lambda k, shape, fan_in: jax.random.normal(k, shape, jnp.float32) * fan_in ** -0.5
    x = jax.random.normal(ks[0], (BATCH, SEQ, D), jnp.float32)
    c = jax.random.normal(ks[1], (BATCH, D), jnp.float32)
    offsets = jax.random.randint(ks[2], (BATCH, 1), 0, 4096, dtype=jnp.int32)
    positions = offsets + jnp.arange(SEQ, dtype=jnp.int32)[None, :]
    w_ada = nrm(ks[3], (DEPTH, D, 3 * D), D) * 0.5
    b_ada = 0.02 * jax.random.normal(ks[4], (DEPTH, 3 * D), jnp.float32)
    norm_pre = 1.0 + 0.1 * jax.random.normal(ks[5], (DEPTH, D), jnp.float32)
    norm_post = 1.0 + 0.1 * jax.random.normal(ks[6], (DEPTH, D), jnp.float32)
    w_in = nrm(ks[7], (DEPTH, D, D_IN_PROJ), D)
    conv_w = nrm(ks[8], (DEPTH, CONV_WIDTH, D_REC), CONV_WIDTH)
    conv_b = 0.02 * jax.random.normal(ks[9], (DEPTH, D_REC), jnp.float32)
    w_rg_a = nrm(ks[10], (DEPTH, N_LRU_BLOCKS, LRU_BLOCK, LRU_BLOCK), LRU_BLOCK)
    b_rg_a = 0.02 * jax.random.normal(ks[11], (DEPTH, D_REC), jnp.float32)
    w_rg_x = nrm(ks[12], (DEPTH, N_LRU_BLOCKS, LRU_BLOCK, LRU_BLOCK), LRU_BLOCK)
    b_rg_x = 0.02 * jax.random.normal(ks[13], (DEPTH, D_REC), jnp.float32)
    a_c = jax.random.uniform(ks[14], (DEPTH, D_REC), jnp.float32, 0.9, 0.999)
    a_base = a_c ** (1.0 / LRU_C)
    lru_lambda = jnp.log(a_base) - jnp.log1p(-a_base)
    norm_rec = 1.0 + 0.1 * jax.random.normal(ks[15], (DEPTH, D_REC), jnp.float32)
    norm_att = 1.0 + 0.1 * jax.random.normal(ks[16], (DEPTH, D_ATT), jnp.float32)
    w_out = nrm(ks[17], (DEPTH, D_MIX, D), D_MIX)
    return {'x': x, 'c': c, 'positions': positions, 'w_ada': w_ada, 'b_ada': b_ada,
            'norm_pre': norm_pre, 'norm_post': norm_post, 'w_in': w_in, 'conv_w': conv_w,
            'conv_b': conv_b, 'w_rg_a': w_rg_a, 'b_rg_a': b_rg_a, 'w_rg_x': w_rg_x,
            'b_rg_x': b_rg_x, 'lru_lambda': lru_lambda, 'norm_rec': norm_rec,
            'norm_att': norm_att, 'w_out': w_out}


def reference(x, c, positions, w_ada, b_ada, norm_pre, norm_post, w_in, conv_w, conv_b,
              w_rg_a, b_rg_a, w_rg_x, b_rg_x, lru_lambda, norm_rec, norm_att, w_out):
    B, S, _ = x.shape
    for l in range(DEPTH):
        mod = jnp.einsum('bd,de->be', jax.nn.silu(c), w_ada[l]) + b_ada[l]
        shift, scale, gate = jnp.split(mod, 3, axis=-1)
        h = rms_norm(x, norm_pre[l]) * (1.0 + scale[:, None, :]) + shift[:, None, :]

        proj = jnp.einsum('bsd,de->bse', h, w_in[l])
        xa, ga, q, k, v, gb = jnp.split(proj, SPLIT_IDX, axis=-1)

        xa = causal_depthwise_conv(xa, conv_w[l], conv_b[l])
        ya = rg_lru(xa, w_rg_a[l], b_rg_a[l], w_rg_x[l], b_rg_x[l], lru_lambda[l]) * jax.nn.silu(ga)
        ya = rms_norm(ya, norm_rec[l])

        heads = lambda t: t.reshape(B, S, N_HEADS, HEAD_DIM)
        qh = (rotary(heads(q), positions) * HEAD_DIM ** -0.5).transpose(0, 2, 1, 3)
        kh = rotary(heads(k), positions).transpose(0, 2, 1, 3)
        vh = heads(v).transpose(0, 2, 1, 3)
        outs, lses = zip(*[dilated_attention(qh, kh, vh, w, d) for (w, d) in DILATED_PATTERNS])
        wts = jax.nn.softmax(jnp.stack(lses, axis=0), axis=0)
        att = jnp.einsum('pbhs,pbhsd->bhsd', wts, jnp.stack(outs, axis=0))
        yb = att.transpose(0, 2, 1, 3).reshape(B, S, D_ATT).astype(x.dtype) * jax.nn.silu(gb)
        yb = rms_norm(yb, norm_att[l])

        mix = jnp.einsum('bse,ed->bsd', jnp.concatenate([ya, yb], axis=-1), w_out[l])
        x = x + gate[:, None, :] * rms_norm(mix, norm_post[l])
    return x
```

```python
import functools

import jax
import jax.numpy as jnp
from jax import lax
from jax.experimental import pallas as pl
from jax.experimental.pallas import tpu as pltpu

F32 = jnp.float32
BF16 = jnp.bfloat16

D_MODEL = 1024
D_REC = 512
D_ATT = 512
N_LRU_BLOCKS = 8
LRU_C = 8.0
CONV_WIDTH = 4
HEAD_DIM = 64
HALF = HEAD_DIM // 2
ROPE_THETA = 10000.0
NORM_EPS = 1e-6
NEG_INF = -1e30
D_IN_PROJ = 2 * D_REC + 4 * D_ATT

LANES = 128
SUBLANES = 8
BLK = 128
DILATIONS = (1, 4, 16)

VMEM_LIMIT = 56 * 1024 * 1024


def _sigmoid(x):
    return 1.0 / (1.0 + jnp.exp(-x))


def _silu(x):
    return x * _sigmoid(x)


def _rms(x, g):
    return x * lax.rsqrt(jnp.mean(x * x, axis=-1, keepdims=True) + NORM_EPS) * g


def _mod_kernel(c_ref, w_ref, b_ref, o_ref):
    c = c_ref[...]
    o_ref[...] = jnp.dot(_silu(c), w_ref[...], preferred_element_type=F32,
                         precision=lax.Precision.HIGHEST) + b_ref[...]


def _mod_call(c, w, b):
    B, D = c.shape
    N = w.shape[1]
    tn = 1024
    return pl.pallas_call(
        _mod_kernel,
        grid=(N // tn,),
        in_specs=[pl.BlockSpec((B, D), lambda j: (0, 0)),
                  pl.BlockSpec((D, tn), lambda j: (0, j)),
                  pl.BlockSpec((1, tn), lambda j: (0, j))],
        out_specs=pl.BlockSpec((B, tn), lambda j: (0, j)),
        out_shape=jax.ShapeDtypeStruct((B, N), F32),
        compiler_params=pltpu.CompilerParams(vmem_limit_bytes=VMEM_LIMIT),
        name="mod",
    )(c, w, b.reshape(1, N))


def _inproj_kernel(x_ref, mod_ref, g_ref, pos_ref, invf_ref, w_ref,
                   xa_ref, ga_ref, q_ref, k_ref, v_ref, gb_ref):
    x = x_ref[0]
    shift = mod_ref[0, :, 0:D_MODEL]
    scale = mod_ref[0, :, D_MODEL:2 * D_MODEL]
    h = _rms(x, g_ref[...]) * (1.0 + scale) + shift
    proj = jnp.dot(h.astype(BF16), w_ref[...], preferred_element_type=F32)

    xa_ref[0] = proj[:, 0:D_REC]
    ga_ref[0] = proj[:, D_REC:2 * D_REC]
    o = 2 * D_REC
    v_ref[0] = proj[:, o + 2 * D_ATT:o + 3 * D_ATT]
    gb_ref[0] = proj[:, o + 3 * D_ATT:o + 4 * D_ATT]

    ang = pos_ref[0].astype(F32) * invf_ref[...]
    cos = jnp.cos(ang)
    sin = jnp.sin(ang)
    lane = lax.broadcasted_iota(jnp.int32, (1, LANES), 1)
    first = (lane % HEAD_DIM) < HALF
    sin_signed = jnp.where(first, -sin, sin)

    def rope(t):
        partner = jnp.where(first, pltpu.roll(t, LANES - HALF, 1), pltpu.roll(t, HALF, 1))
        return t * cos + partner * sin_signed

    for j in range(D_ATT // LANES):
        sl = slice(j * LANES, (j + 1) * LANES)
        q_ref[0, :, sl] = rope(proj[:, o + j * LANES:o + (j + 1) * LANES]) * (HEAD_DIM ** -0.5)
        k_ref[0, :, sl] = rope(proj[:, o + D_ATT + j * LANES:o + D_ATT + (j + 1) * LANES])


def _inproj_call(x, mod, g, pos, invf, w, ts):
    B, S, D = x.shape
    seq = lambda b, i: (b, i, 0)
    const = lambda b, i: (0, 0)
    half = pl.BlockSpec((1, ts, D_REC), seq)
    half_shape = jax.ShapeDtypeStruct((B, S, D_REC), F32)
    return pl.pallas_call(
        _inproj_kernel,
        grid=(B, S // ts),
        in_specs=[pl.BlockSpec((1, ts, D), seq),
                  pl.BlockSpec((1, 1, 3 * D), lambda b, i: (b, 0, 0)),
                  pl.BlockSpec((1, D), const),
                  pl.BlockSpec((1, ts, 1), seq),
                  pl.BlockSpec((1, LANES), const),
                  pl.BlockSpec((D, D_IN_PROJ), const)],
        out_specs=[half] * 6,
        out_shape=[half_shape] * 6,
        compiler_params=pltpu.CompilerParams(
            dimension_semantics=("parallel", "parallel"), vmem_limit_bytes=VMEM_LIMIT),
        name="inproj",
    )(x, mod.reshape(B, 1, 3 * D), g.reshape(1, D), pos.reshape(B, S, 1), invf, w)


def _rec_kernel(xa_ref, ga_ref, cw_ref, cb_ref, wa_ref, wx_ref, ba_ref, bx_ref, lam_ref, g_ref,
                o_ref, xs, a_s, u_s, h_s, *, T):
    B = SUBLANES
    tail = (CONV_WIDTH - 1) * B
    n_slab = D_REC // LANES

    @pl.when(pl.program_id(0) == 0)
    def _():
        xs[:, 0:tail, :] = jnp.zeros((n_slab, tail, LANES), F32)
        h_s[...] = jnp.zeros((n_slab, B, LANES), F32)

    for b in range(B):
        for j in range(n_slab):
            xs[j, pl.ds(tail + b, T, stride=B), :] = xa_ref[b, :, j * LANES:(j + 1) * LANES]

    xc = jnp.concatenate(
        [sum(cw_ref[k:k + 1, j * LANES:(j + 1) * LANES] * xs[j, k * B:k * B + T * B, :]
             for k in range(CONV_WIDTH)) for j in range(n_slab)], axis=-1) + cb_ref[...]
    xs[:, 0:tail, :] = xs[:, T * B:T * B + tail, :]

    xcb = xc.astype(BF16)
    r = _sigmoid(jnp.dot(xcb, wa_ref[...], preferred_element_type=F32) + ba_ref[...])
    ig = _sigmoid(jnp.dot(xcb, wx_ref[...], preferred_element_type=F32) + bx_ref[...])
    z = -lam_ref[...]
    softplus = jnp.maximum(z, 0.0) + jnp.log1p(jnp.exp(-jnp.abs(z)))
    log_a = (-LRU_C) * r * softplus
    a = jnp.exp(log_a)
    u = jnp.sqrt(-jnp.tanh(log_a) * (1.0 + a * a)) * (ig * xc)
    for j in range(n_slab):
        a_s[j] = a[:, j * LANES:(j + 1) * LANES]
        u_s[j] = u[:, j * LANES:(j + 1) * LANES]

    def step(t, h):
        rows = pl.ds(pl.multiple_of(t * B, B), B)
        h = a_s[:, rows, :] * h + u_s[:, rows, :]
        u_s[:, rows, :] = h
        return h

    h_s[...] = lax.fori_loop(0, T, step, h_s[...], unroll=8)

    for b in range(B):
        hb = jnp.concatenate([u_s[j, pl.ds(b, T, stride=B), :] for j in range(n_slab)], axis=-1)
        o_ref[b] = _rms(hb * _silu(ga_ref[b]), g_ref[...]).astype(o_ref.dtype)


def _rec_call(xa, ga, cw, cb, wa, wx, ba, bx, lam, g, T):
    B, S, C = xa.shape
    assert B == SUBLANES
    seq = pl.BlockSpec((B, T, C), lambda i: (0, i, 0))
    vec = pl.BlockSpec((1, C), lambda i: (0, 0))
    mat = pl.BlockSpec((C, C), lambda i: (0, 0))
    return pl.pallas_call(
        functools.partial(_rec_kernel, T=T),
        grid=(S // T,),
        in_specs=[seq, seq, pl.BlockSpec((CONV_WIDTH, C), lambda i: (0, 0)), vec, mat, mat,
                  vec, vec, vec, vec],
        out_specs=seq,
        out_shape=jax.ShapeDtypeStruct((B, S, C), BF16),
        scratch_shapes=[pltpu.VMEM((C // LANES, (T + CONV_WIDTH - 1) * B, LANES), F32),
                        pltpu.VMEM((C // LANES, T * B, LANES), F32),
                        pltpu.VMEM((C // LANES, T * B, LANES), F32),
                        pltpu.VMEM((C // LANES, B, LANES), F32)],
        compiler_params=pltpu.CompilerParams(
            dimension_semantics=("arbitrary",), vmem_limit_bytes=VMEM_LIMIT),
        name="rec",
    )(xa, ga, cw, cb.reshape(1, C), wa, wx, ba.reshape(1, C), bx.reshape(1, C),
      lam.reshape(1, C), g.reshape(1, C))


def _attn_kernel(q_ref, k_ref, v_ref, o_ref, bias_s, acc_s, m_s, l_s, *, S):
    qi = lax.broadcasted_iota(jnp.int32, (BLK, 2 * BLK), 0)
    ki = lax.broadcasted_iota(jnp.int32, (BLK, 2 * BLK), 1)
    dist = qi + BLK - ki
    bias_s[...] = jnp.where((dist >= 0) & (dist <= BLK), 0.0, NEG_INF).astype(F32)
    lane = lax.broadcasted_iota(jnp.int32, (1, LANES), 1)
    head_a = lane < HEAD_DIM

    def tile(p, q_rows, k_rows, nk):
        qt = q_ref[0, q_rows, :]
        kt = k_ref[0, k_rows, :].astype(BF16)
        vt = v_ref[0, k_rows, :].astype(BF16)
        bias = bias_s[:, 2 * BLK - nk:2 * BLK]
        accs, ms, ls = [], [], []
        for sel in (head_a, ~head_a):
            qh = jnp.where(sel, qt, 0.0).astype(BF16)
            s = lax.dot_general(qh, kt, (((1,), (1,)), ((), ())), preferred_element_type=F32) + bias
            m = jnp.max(s, axis=-1, keepdims=True)
            e = jnp.exp(s - m)
            ls.append(jnp.sum(e, axis=-1, keepdims=True))
            ms.append(m)
            accs.append(jnp.dot(e.astype(BF16), vt, preferred_element_type=F32))
        acc_s[p, q_rows, :] = jnp.where(head_a, accs[0], accs[1])
        m_s[p, q_rows, :] = jnp.where(head_a, ms[0], ms[1])
        l_s[p, q_rows, :] = jnp.where(head_a, ls[0], ls[1])

    for p, d in enumerate(DILATIONS):
        n_blk = S // (d * BLK)
        for r in range(d):
            tile(p, pl.ds(r, BLK, stride=d), pl.ds(r, BLK, stride=d), BLK)
            if n_blk > 1:
                def body(n, carry, p=p, d=d, r=r):
                    q0 = r + n * (d * BLK)
                    tile(p, pl.ds(q0, BLK, stride=d), pl.ds(q0 - d * BLK, 2 * BLK, stride=d), 2 * BLK)
                    return carry
                lax.fori_loop(1, n_blk, body, 0)

    rows_per = 256

    def combine(c, carry):
        rows = pl.ds(pl.multiple_of(c * rows_per, rows_per), rows_per)
        m = [m_s[p, rows, :] for p in range(3)]
        mx = jnp.maximum(jnp.maximum(m[0], m[1]), m[2])
        w = [jnp.exp(mp - mx) for mp in m]
        num = sum(w[p] * acc_s[p, rows, :] for p in range(3))
        den = sum(w[p] * l_s[p, rows, :] for p in range(3))
        o_ref[0, rows, :] = (num / den).astype(o_ref.dtype)
        return carry

    lax.fori_loop(0, S // rows_per, combine, 0)


def _attn_call(q, k, v):
    B, S, C = q.shape
    spec = pl.BlockSpec((1, S, LANES), lambda b, p: (b, 0, p))
    return pl.pallas_call(
        functools.partial(_attn_kernel, S=S),
        grid=(B, C // LANES),
        in_specs=[spec, spec, spec],
        out_specs=spec,
        out_shape=jax.ShapeDtypeStruct((B, S, C), F32),
        scratch_shapes=[pltpu.VMEM((BLK, 2 * BLK), F32),
                        pltpu.VMEM((3, S, LANES), F32),
                        pltpu.VMEM((3, S, LANES), F32),
                        pltpu.VMEM((3, S, LANES), F32)],
        compiler_params=pltpu.CompilerParams(
            dimension_semantics=("parallel", "parallel"), vmem_limit_bytes=VMEM_LIMIT),
        name="attn",
    )(q, k, v)


def _outproj_kernel(x_ref, ya_ref, att_ref, gb_ref, mod_ref, natt_ref, npost_ref, w1_ref, w2_ref,
                    o_ref):
    yb = _rms(att_ref[0] * _silu(gb_ref[0]), natt_ref[...])
    mix = (jnp.dot(ya_ref[0], w1_ref[...], preferred_element_type=F32)
           + jnp.dot(yb.astype(BF16), w2_ref[...], preferred_element_type=F32))
    gate = mod_ref[0, :, 2 * D_MODEL:3 * D_MODEL]
    o_ref[0] = x_ref[0] + gate * _rms(mix, npost_ref[...])


def _outproj_call(x, ya, att, gb, mod, natt, npost, w1, w2, ts):
    B, S, D = x.shape
    seq = lambda b, i: (b, i, 0)
    const = lambda b, i: (0, 0)
    half = pl.BlockSpec((1, ts, D_REC), seq)
    return pl.pallas_call(
        _outproj_kernel,
        grid=(B, S // ts),
        in_specs=[pl.BlockSpec((1, ts, D), seq), half, half, half,
                  pl.BlockSpec((1, 1, 3 * D), lambda b, i: (b, 0, 0)),
                  pl.BlockSpec((1, D_ATT), const),
                  pl.BlockSpec((1, D), const),
                  pl.BlockSpec((D_REC, D), const),
                  pl.BlockSpec((D_ATT, D), const)],
        out_specs=pl.BlockSpec((1, ts, D), seq),
        out_shape=jax.ShapeDtypeStruct((B, S, D), F32),
        compiler_params=pltpu.CompilerParams(
            dimension_semantics=("parallel", "parallel"), vmem_limit_bytes=VMEM_LIMIT),
        name="outproj",
    )(x, ya, att, gb, mod.reshape(B, 1, 3 * D), natt.reshape(1, D_ATT), npost.reshape(1, D), w1, w2)


def _block_diag(w):
    nb, n, _ = w.shape
    eye = jnp.eye(nb, dtype=w.dtype)
    return jnp.einsum('hij,hg->higj', w, eye).reshape(nb * n, nb * n)


def kernel(x, c, positions, w_ada, b_ada, norm_pre, norm_post, w_in, conv_w, conv_b, w_rg_a, b_rg_a,
           w_rg_x, b_rg_x, lru_lambda, norm_rec, norm_att, w_out):
    depth = w_in.shape[0]
    inv_freq = ROPE_THETA ** (-jnp.arange(HALF, dtype=F32) / HALF)
    invf = jnp.tile(inv_freq, LANES // HALF).reshape(1, LANES)
    for l in range(depth):
        mod = _mod_call(c, w_ada[l], b_ada[l])
        xa, ga, q, k, v, gb = _inproj_call(x, mod, norm_pre[l], positions, invf,
                                           w_in[l].astype(BF16), ts=256)
        ya = _rec_call(xa, ga, conv_w[l], conv_b[l],
                       _block_diag(w_rg_a[l]).astype(BF16), _block_diag(w_rg_x[l]).astype(BF16),
                       b_rg_a[l], b_rg_x[l], lru_lambda[l], norm_rec[l], T=64)
        att = _attn_call(q, k, v)
        x = _outproj_call(x, ya, att, gb, mod, norm_att[l], norm_post[l],
                          w_out[l, :D_REC].astype(BF16), w_out[l, D_REC:].astype(BF16), ts=256)
    return x
```

```python
import functools

import jax
import jax.numpy as jnp
from jax import lax
from jax.experimental import pallas as pl
from jax.experimental.pallas import tpu as pltpu

F32 = jnp.float32
BF16 = jnp.bfloat16

D_MODEL = 1024
D_REC = 512
D_ATT = 512
N_LRU_BLOCKS = 8
LRU_C = 8.0
CONV_WIDTH = 4
HEAD_DIM = 64
HALF = HEAD_DIM // 2
ROPE_THETA = 10000.0
NORM_EPS = 1e-6
NEG_INF = -1e30
D_IN_PROJ = 2 * D_REC + 4 * D_ATT

LANES = 128
SUBLANES = 8
BLK = 128
DILATIONS = (1, 4, 16)

VMEM_LIMIT = 56 * 1024 * 1024
Q_SCALE = HEAD_DIM ** -0.5 * 1.4426950408889634


def _sigmoid(x):
    return 1.0 / (1.0 + jnp.exp(-x))


def _silu(x):
    return x * _sigmoid(x)


def _rms(x, g):
    return x * lax.rsqrt(jnp.mean(x * x, axis=-1, keepdims=True) + NORM_EPS) * g


def _mod_kernel(c_ref, w_ref, b_ref, o_ref):
    c = c_ref[...]
    o_ref[...] = jnp.dot(_silu(c), w_ref[...], preferred_element_type=F32,
                         precision=lax.Precision.HIGHEST) + b_ref[...]


def _mod_call(c, w, b):
    B, D = c.shape
    N = w.shape[1]
    tn = 1024
    return pl.pallas_call(
        _mod_kernel,
        grid=(N // tn,),
        in_specs=[pl.BlockSpec((B, D), lambda j: (0, 0)),
                  pl.BlockSpec((D, tn), lambda j: (0, j)),
                  pl.BlockSpec((1, tn), lambda j: (0, j))],
        out_specs=pl.BlockSpec((B, tn), lambda j: (0, j)),
        out_shape=jax.ShapeDtypeStruct((B, N), F32),
        compiler_params=pltpu.CompilerParams(vmem_limit_bytes=VMEM_LIMIT),
        name="mod",
    )(c, w, b.reshape(1, N))


def _inproj_kernel(x_ref, mod_ref, g_ref, pos_ref, invf_ref, w_ref,
                   xa_ref, ga_ref, q_ref, k_ref, v_ref, gb_ref):
    x = x_ref[0]
    shift = mod_ref[0, :, 0:D_MODEL]
    scale = mod_ref[0, :, D_MODEL:2 * D_MODEL]
    h = _rms(x, g_ref[...]) * (1.0 + scale) + shift
    proj = jnp.dot(h.astype(BF16), w_ref[...], preferred_element_type=F32)

    xa_ref[0] = proj[:, 0:D_REC]
    ga_ref[0] = proj[:, D_REC:2 * D_REC]
    o = 2 * D_REC
    v_ref[0] = proj[:, o + 2 * D_ATT:o + 3 * D_ATT]
    gb_ref[0] = proj[:, o + 3 * D_ATT:o + 4 * D_ATT]

    ang = pos_ref[0].astype(F32) * invf_ref[...]
    cos = jnp.cos(ang)
    sin = jnp.sin(ang)
    lane = lax.broadcasted_iota(jnp.int32, (1, LANES), 1)
    first = (lane % HEAD_DIM) < HALF
    sin_signed = jnp.where(first, -sin, sin)

    def rope(t):
        partner = jnp.where(first, pltpu.roll(t, LANES - HALF, 1), pltpu.roll(t, HALF, 1))
        return t * cos + partner * sin_signed

    for j in range(D_ATT // LANES):
        sl = slice(j * LANES, (j + 1) * LANES)
        q_ref[0, :, sl] = rope(proj[:, o + j * LANES:o + (j + 1) * LANES]) * Q_SCALE
        k_ref[0, :, sl] = rope(proj[:, o + D_ATT + j * LANES:o + D_ATT + (j + 1) * LANES])


def _inproj_call(x, mod, g, pos, invf, w, ts):
    B, S, D = x.shape
    seq = lambda b, i: (b, i, 0)
    const = lambda b, i: (0, 0)
    half = pl.BlockSpec((1, ts, D_REC), seq)
    half_shape = jax.ShapeDtypeStruct((B, S, D_REC), F32)
    return pl.pallas_call(
        _inproj_kernel,
        grid=(B, S // ts),
        in_specs=[pl.BlockSpec((1, ts, D), seq),
                  pl.BlockSpec((1, 1, 3 * D), lambda b, i: (b, 0, 0)),
                  pl.BlockSpec((1, D), const),
                  pl.BlockSpec((1, ts, 1), seq),
                  pl.BlockSpec((1, LANES), const),
                  pl.BlockSpec((D, D_IN_PROJ), const)],
        out_specs=[half] * 6,
        out_shape=[half_shape] * 6,
        compiler_params=pltpu.CompilerParams(
            dimension_semantics=("parallel", "parallel"), vmem_limit_bytes=VMEM_LIMIT),
        name="inproj",
    )(x, mod.reshape(B, 1, 3 * D), g.reshape(1, D), pos.reshape(B, S, 1), invf, w)


def _rec_kernel(xa_ref, ga_ref, cw_ref, cb_ref, wa_ref, wx_ref, ba_ref, bx_ref, lam_ref, g_ref,
                o_ref, xs, a_s, u_s, h_s, *, T):
    B = SUBLANES
    tail = (CONV_WIDTH - 1) * B
    n_slab = D_REC // LANES

    @pl.when(pl.program_id(0) == 0)
    def _():
        xs[:, 0:tail, :] = jnp.zeros((n_slab, tail, LANES), F32)
        h_s[...] = jnp.zeros((n_slab, B, LANES), F32)

    for b in range(B):
        for j in range(n_slab):
            xs[j, pl.ds(tail + b, T, stride=B), :] = xa_ref[b, :, j * LANES:(j + 1) * LANES]

    xc = jnp.concatenate(
        [sum(cw_ref[k:k + 1, j * LANES:(j + 1) * LANES] * xs[j, k * B:k * B + T * B, :]
             for k in range(CONV_WIDTH)) for j in range(n_slab)], axis=-1) + cb_ref[...]
    xs[:, 0:tail, :] = xs[:, T * B:T * B + tail, :]

    xcb = xc.astype(BF16)
    r = _sigmoid(jnp.dot(xcb, wa_ref[...], preferred_element_type=F32) + ba_ref[...])
    ig = _sigmoid(jnp.dot(xcb, wx_ref[...], preferred_element_type=F32) + bx_ref[...])
    z = -lam_ref[...]
    softplus = jnp.maximum(z, 0.0) + jnp.log1p(jnp.exp(-jnp.abs(z)))
    log_a = (-LRU_C) * r * softplus
    a = jnp.exp(log_a)
    u = jnp.sqrt(-jnp.tanh(log_a) * (1.0 + a * a)) * (ig * xc)
    for j in range(n_slab):
        a_s[j] = a[:, j * LANES:(j + 1) * LANES]
        u_s[j] = u[:, j * LANES:(j + 1) * LANES]

    def step(t, h):
        rows = pl.ds(pl.multiple_of(t * B, B), B)
        h = a_s[:, rows, :] * h + u_s[:, rows, :]
        u_s[:, rows, :] = h
        return h

    h_s[...] = lax.fori_loop(0, T, step, h_s[...], unroll=8)

    for b in range(B):
        hb = jnp.concatenate([u_s[j, pl.ds(b, T, stride=B), :] for j in range(n_slab)], axis=-1)
        o_ref[b] = _rms(hb * _silu(ga_ref[b]), g_ref[...]).astype(o_ref.dtype)


def _rec_call(xa, ga, cw, cb, wa, wx, ba, bx, lam, g, T):
    B, S, C = xa.shape
    assert B == SUBLANES
    seq = pl.BlockSpec((B, T, C), lambda i: (0, i, 0))
    vec = pl.BlockSpec((1, C), lambda i: (0, 0))
    mat = pl.BlockSpec((C, C), lambda i: (0, 0))
    return pl.pallas_call(
        functools.partial(_rec_kernel, T=T),
        grid=(S // T,),
        in_specs=[seq, seq, pl.BlockSpec((CONV_WIDTH, C), lambda i: (0, 0)), vec, mat, mat,
                  vec, vec, vec, vec],
        out_specs=seq,
        out_shape=jax.ShapeDtypeStruct((B, S, C), BF16),
        scratch_shapes=[pltpu.VMEM((C // LANES, (T + CONV_WIDTH - 1) * B, LANES), F32),
                        pltpu.VMEM((C // LANES, T * B, LANES), F32),
                        pltpu.VMEM((C // LANES, T * B, LANES), F32),
                        pltpu.VMEM((C // LANES, B, LANES), F32)],
        compiler_params=pltpu.CompilerParams(
            dimension_semantics=("arbitrary",), vmem_limit_bytes=VMEM_LIMIT),
        name="rec",
    )(xa, ga, cw, cb.reshape(1, C), wa, wx, ba.reshape(1, C), bx.reshape(1, C),
      lam.reshape(1, C), g.reshape(1, C))


def _attn_kernel(q_ref, k_ref, v_ref, o_ref, bias_s, acc_s, m_s, l_s, *, S):
    qi = lax.broadcasted_iota(jnp.int32, (2 * BLK, 2 * BLK), 0) % BLK
    ki = lax.broadcasted_iota(jnp.int32, (2 * BLK, 2 * BLK), 1)
    dist = qi + BLK - ki
    bias_s[...] = jnp.where((dist >= 0) & (dist <= BLK), 0.0, NEG_INF).astype(F32)
    lane = lax.broadcasted_iota(jnp.int32, (1, LANES), 1)
    head_a = lane < HEAD_DIM

    def tile(p, d, q0, first):
        if d == 1:
            q_rows = pl.ds(pl.multiple_of(q0, BLK), BLK)
            k_rows = q_rows if first else pl.ds(pl.multiple_of(q0 - BLK, BLK), 2 * BLK)
        else:
            q_rows = pl.ds(q0, BLK, stride=d)
            k_rows = q_rows if first else pl.ds(q0 - d * BLK, 2 * BLK, stride=d)
        nk = BLK if first else 2 * BLK
        qt = q_ref[0, q_rows, :]
        kt = k_ref[0, k_rows, :].astype(BF16)
        vt = v_ref[0, k_rows, :].astype(BF16)
        q2 = jnp.concatenate([jnp.where(head_a, qt, 0.0), jnp.where(head_a, 0.0, qt)], axis=0)
        s = lax.dot_general(q2.astype(BF16), kt, (((1,), (1,)), ((), ())),
                            preferred_element_type=F32) + bias_s[:, 2 * BLK - nk:2 * BLK]
        m = jnp.max(s, axis=-1, keepdims=True)
        e = jnp.exp2(s - m).astype(BF16)
        v_aug = jnp.concatenate([vt, jnp.ones((nk, LANES), BF16)], axis=1)
        r = jnp.dot(e, v_aug, preferred_element_type=F32)
        acc_s[p, q_rows, :] = jnp.where(head_a, r[0:BLK, 0:LANES], r[BLK:, 0:LANES])
        l_s[p, q_rows, :] = jnp.where(head_a, r[0:BLK, LANES:], r[BLK:, LANES:])
        m_s[p, q_rows, :] = jnp.where(head_a, m[0:BLK], m[BLK:])

    def groups(n_groups, per_group, emit):
        def body(c, carry):
            for g in range(per_group):
                emit(c, g)
            return carry
        if n_groups == 1:
            body(0, 0)
        else:
            lax.fori_loop(0, n_groups, body, 0)

    tile(0, 1, 0, True)
    groups(5, 3, lambda c, g: tile(0, 1, (1 + 3 * c + g) * BLK, False))
    groups(1, 4, lambda c, g: tile(1, 4, g, True))
    groups(3, 4, lambda c, g: tile(1, 4, g + (c + 1) * 4 * BLK, False))
    groups(4, 4, lambda c, g: tile(2, 16, 4 * c + g, True))

    rows_per = 256

    def combine(c, carry):
        rows = pl.ds(pl.multiple_of(c * rows_per, rows_per), rows_per)
        m = [m_s[p, rows, :] for p in range(3)]
        mx = jnp.maximum(jnp.maximum(m[0], m[1]), m[2])
        w = [jnp.exp2(mp - mx) for mp in m]
        num = sum(w[p] * acc_s[p, rows, :] for p in range(3))
        den = sum(w[p] * l_s[p, rows, :] for p in range(3))
        o_ref[0, rows, :] = (num / den).astype(o_ref.dtype)
        return carry

    lax.fori_loop(0, S // rows_per, combine, 0)


def _attn_call(q, k, v):
    B, S, C = q.shape
    spec = pl.BlockSpec((1, S, LANES), lambda b, p: (b, 0, p))
    return pl.pallas_call(
        functools.partial(_attn_kernel, S=S),
        grid=(B, C // LANES),
        in_specs=[spec, spec, spec],
        out_specs=spec,
        out_shape=jax.ShapeDtypeStruct((B, S, C), F32),
        scratch_shapes=[pltpu.VMEM((2 * BLK, 2 * BLK), F32),
                        pltpu.VMEM((3, S, LANES), F32),
                        pltpu.VMEM((3, S, LANES), F32),
                        pltpu.VMEM((3, S, LANES), F32)],
        compiler_params=pltpu.CompilerParams(
            dimension_semantics=("parallel", "parallel"), vmem_limit_bytes=VMEM_LIMIT),
        name="attn",
    )(q, k, v)


def _outproj_kernel(x_ref, ya_ref, att_ref, gb_ref, mod_ref, natt_ref, npost_ref, w1_ref, w2_ref,
                    o_ref):
    yb = _rms(att_ref[0] * _silu(gb_ref[0]), natt_ref[...])
    mix = (jnp.dot(ya_ref[0], w1_ref[...], preferred_element_type=F32)
           + jnp.dot(yb.astype(BF16), w2_ref[...], preferred_element_type=F32))
    gate = mod_ref[0, :, 2 * D_MODEL:3 * D_MODEL]
    o_ref[0] = x_ref[0] + gate * _rms(mix, npost_ref[...])


def _outproj_call(x, ya, att, gb, mod, natt, npost, w1, w2, ts):
    B, S, D = x.shape
    seq = lambda b, i: (b, i, 0)
    const = lambda b, i: (0, 0)
    half = pl.BlockSpec((1, ts, D_REC), seq)
    return pl.pallas_call(
        _outproj_kernel,
        grid=(B, S // ts),
        in_specs=[pl.BlockSpec((1, ts, D), seq), half, half, half,
                  pl.BlockSpec((1, 1, 3 * D), lambda b, i: (b, 0, 0)),
                  pl.BlockSpec((1, D_ATT), const),
                  pl.BlockSpec((1, D), const),
                  pl.BlockSpec((D_REC, D), const),
                  pl.BlockSpec((D_ATT, D), const)],
        out_specs=pl.BlockSpec((1, ts, D), seq),
        out_shape=jax.ShapeDtypeStruct((B, S, D), F32),
        compiler_params=pltpu.CompilerParams(
            dimension_semantics=("parallel", "parallel"), vmem_limit_bytes=VMEM_LIMIT),
        name="outproj",
    )(x, ya, att, gb, mod.reshape(B, 1, 3 * D), natt.reshape(1, D_ATT), npost.reshape(1, D), w1, w2)


def _block_diag(w):
    nb, n, _ = w.shape
    eye = jnp.eye(nb, dtype=w.dtype)
    return jnp.einsum('hij,hg->higj', w, eye).reshape(nb * n, nb * n)


def kernel(x, c, positions, w_ada, b_ada, norm_pre, norm_post, w_in, conv_w, conv_b, w_rg_a, b_rg_a,
           w_rg_x, b_rg_x, lru_lambda, norm_rec, norm_att, w_out):
    depth = w_in.shape[0]
    inv_freq = ROPE_THETA ** (-jnp.arange(HALF, dtype=F32) / HALF)
    invf = jnp.tile(inv_freq, LANES // HALF).reshape(1, LANES)
    for l in range(depth):
        mod = _mod_call(c, w_ada[l], b_ada[l])
        xa, ga, q, k, v, gb = _inproj_call(x, mod, norm_pre[l], positions, invf,
                                           w_in[l].astype(BF16), ts=256)
        ya = _rec_call(xa, ga, conv_w[l], conv_b[l],
                       _block_diag(w_rg_a[l]).astype(BF16), _block_diag(w_rg_x[l]).astype(BF16),
                       b_rg_a[l], b_rg_x[l], lru_lambda[l], norm_rec[l], T=64)
        att = _attn_call(q, k, v)
        x = _outproj_call(x, ya, att, gb, mod, norm_att[l], norm_post[l],
                          w_out[l, :D_REC].astype(BF16), w_out[l, D_REC:].astype(BF16), ts=256)
    return x
```

```python
import functools

import jax
import jax.numpy as jnp
from jax import lax
from jax.experimental import pallas as pl
from jax.experimental.pallas import tpu as pltpu

F32 = jnp.float32
BF16 = jnp.bfloat16

D_MODEL = 1024
D_REC = 512
D_ATT = 512
N_LRU_BLOCKS = 8
LRU_C = 8.0
CONV_WIDTH = 4
HEAD_DIM = 64
HALF = HEAD_DIM // 2
ROPE_THETA = 10000.0
NORM_EPS = 1e-6
NEG_INF = -1e30
D_IN_PROJ = 2 * D_REC + 4 * D_ATT

LANES = 128
SUBLANES = 8
BLK = 128
DILATIONS = (1, 4, 16)

VMEM_LIMIT = 56 * 1024 * 1024
Q_SCALE = HEAD_DIM ** -0.5 * 1.4426950408889634


def _sigmoid(x):
    return 1.0 / (1.0 + jnp.exp(-x))


def _silu(x):
    return x * _sigmoid(x)


def _rms(x, g):
    return x * lax.rsqrt(jnp.mean(x * x, axis=-1, keepdims=True) + NORM_EPS) * g


def _mod_kernel(c_ref, w_ref, b_ref, o_ref):
    c = c_ref[...]
    o_ref[...] = jnp.dot(_silu(c), w_ref[...], preferred_element_type=F32,
                         precision=lax.Precision.HIGHEST) + b_ref[...]


def _mod_call(c, w, b):
    B, D = c.shape
    N = w.shape[1]
    tn = 1024
    return pl.pallas_call(
        _mod_kernel,
        grid=(N // tn,),
        in_specs=[pl.BlockSpec((B, D), lambda j: (0, 0)),
                  pl.BlockSpec((D, tn), lambda j: (0, j)),
                  pl.BlockSpec((1, tn), lambda j: (0, j))],
        out_specs=pl.BlockSpec((B, tn), lambda j: (0, j)),
        out_shape=jax.ShapeDtypeStruct((B, N), F32),
        compiler_params=pltpu.CompilerParams(vmem_limit_bytes=VMEM_LIMIT),
        name="mod",
    )(c, w, b.reshape(1, N))


def _inproj_kernel(x_ref, mod_ref, g_ref, pos_ref, invf_ref, w_ref,
                   xa_ref, ga_ref, q_ref, k_ref, v_ref, gb_ref):
    x = x_ref[0]
    shift = mod_ref[0, :, 0:D_MODEL]
    scale = mod_ref[0, :, D_MODEL:2 * D_MODEL]
    h = _rms(x, g_ref[...]) * (1.0 + scale) + shift
    proj = jnp.dot(h.astype(BF16), w_ref[...], preferred_element_type=F32)

    xa_ref[0] = proj[:, 0:D_REC]
    ga_ref[0] = proj[:, D_REC:2 * D_REC]
    o = 2 * D_REC
    v_ref[0] = proj[:, o + 2 * D_ATT:o + 3 * D_ATT]
    gb_ref[0] = proj[:, o + 3 * D_ATT:o + 4 * D_ATT]

    ang = pos_ref[0].astype(F32) * invf_ref[...]
    cos = jnp.cos(ang)
    sin = jnp.sin(ang)
    lane = lax.broadcasted_iota(jnp.int32, (1, LANES), 1)
    first = (lane % HEAD_DIM) < HALF
    sin_signed = jnp.where(first, -sin, sin)

    def rope(t):
        partner = jnp.where(first, pltpu.roll(t, LANES - HALF, 1), pltpu.roll(t, HALF, 1))
        return t * cos + partner * sin_signed

    for j in range(D_ATT // LANES):
        sl = slice(j * LANES, (j + 1) * LANES)
        q_ref[0, :, sl] = rope(proj[:, o + j * LANES:o + (j + 1) * LANES]) * Q_SCALE
        k_ref[0, :, sl] = rope(proj[:, o + D_ATT + j * LANES:o + D_ATT + (j + 1) * LANES])


def _inproj_call(x, mod, g, pos, invf, w, ts):
    B, S, D = x.shape
    seq = lambda b, i: (b, i, 0)
    const = lambda b, i: (0, 0)
    half = pl.BlockSpec((1, ts, D_REC), seq)
    half_shape = jax.ShapeDtypeStruct((B, S, D_REC), F32)
    return pl.pallas_call(
        _inproj_kernel,
        grid=(B, S // ts),
        in_specs=[pl.BlockSpec((1, ts, D), seq),
                  pl.BlockSpec((1, 1, 3 * D), lambda b, i: (b, 0, 0)),
                  pl.BlockSpec((1, D), const),
                  pl.BlockSpec((1, ts, 1), seq),
                  pl.BlockSpec((1, LANES), const),
                  pl.BlockSpec((D, D_IN_PROJ), const)],
        out_specs=[half] * 6,
        out_shape=[half_shape] * 6,
        compiler_params=pltpu.CompilerParams(
            dimension_semantics=("parallel", "parallel"), vmem_limit_bytes=VMEM_LIMIT),
        name="inproj",
    )(x, mod.reshape(B, 1, 3 * D), g.reshape(1, D), pos.reshape(B, S, 1), invf, w)


def _rec_kernel(xa_ref, ga_ref, cw_ref, cb_ref, wa_ref, wx_ref, ba_ref, bx_ref, lam_ref, g_ref,
                o_ref, xs, a_s, u_s, h_s, *, T):
    B = SUBLANES
    tail = (CONV_WIDTH - 1) * B
    n_slab = D_REC // LANES

    @pl.when(pl.program_id(0) == 0)
    def _():
        xs[:, 0:tail, :] = jnp.zeros((n_slab, tail, LANES), F32)
        h_s[...] = jnp.zeros((n_slab, B, LANES), F32)

    for b in range(B):
        for j in range(n_slab):
            xs[j, pl.ds(tail + b, T, stride=B), :] = xa_ref[b, :, j * LANES:(j + 1) * LANES]

    xc = jnp.concatenate(
        [sum(cw_ref[k:k + 1, j * LANES:(j + 1) * LANES] * xs[j, k * B:k * B + T * B, :]
             for k in range(CONV_WIDTH)) for j in range(n_slab)], axis=-1) + cb_ref[...]
    xs[:, 0:tail, :] = xs[:, T * B:T * B + tail, :]

    xcb = xc.astype(BF16)
    r = _sigmoid(jnp.dot(xcb, wa_ref[...], preferred_element_type=F32) + ba_ref[...])
    ig = _sigmoid(jnp.dot(xcb, wx_ref[...], preferred_element_type=F32) + bx_ref[...])
    z = -lam_ref[...]
    softplus = jnp.maximum(z, 0.0) + jnp.log1p(jnp.exp(-jnp.abs(z)))
    log_a = (-LRU_C) * r * softplus
    a = jnp.exp(log_a)
    u = jnp.sqrt(-jnp.tanh(log_a) * (1.0 + a * a)) * (ig * xc)
    for j in range(n_slab):
        a_s[j] = a[:, j * LANES:(j + 1) * LANES]
        u_s[j] = u[:, j * LANES:(j + 1) * LANES]

    def step(t, h):
        rows = pl.ds(pl.multiple_of(t * B, B), B)
        h = a_s[:, rows, :] * h + u_s[:, rows, :]
        u_s[:, rows, :] = h
        return h

    h_s[...] = lax.fori_loop(0, T, step, h_s[...], unroll=8)

    for b in range(B):
        hb = jnp.concatenate([u_s[j, pl.ds(b, T, stride=B), :] for j in range(n_slab)], axis=-1)
        o_ref[b] = _rms(hb * _silu(ga_ref[b]), g_ref[...]).astype(o_ref.dtype)


def _rec_call(xa, ga, cw, cb, wa, wx, ba, bx, lam, g, T):
    B, S, C = xa.shape
    assert B == SUBLANES
    seq = pl.BlockSpec((B, T, C), lambda i: (0, i, 0))
    vec = pl.BlockSpec((1, C), lambda i: (0, 0))
    mat = pl.BlockSpec((C, C), lambda i: (0, 0))
    return pl.pallas_call(
        functools.partial(_rec_kernel, T=T),
        grid=(S // T,),
        in_specs=[seq, seq, pl.BlockSpec((CONV_WIDTH, C), lambda i: (0, 0)), vec, mat, mat,
                  vec, vec, vec, vec],
        out_specs=seq,
        out_shape=jax.ShapeDtypeStruct((B, S, C), BF16),
        scratch_shapes=[pltpu.VMEM((C // LANES, (T + CONV_WIDTH - 1) * B, LANES), F32),
                        pltpu.VMEM((C // LANES, T * B, LANES), F32),
                        pltpu.VMEM((C // LANES, T * B, LANES), F32),
                        pltpu.VMEM((C // LANES, B, LANES), F32)],
        compiler_params=pltpu.CompilerParams(
            dimension_semantics=("arbitrary",), vmem_limit_bytes=VMEM_LIMIT),
        name="rec",
    )(xa, ga, cw, cb.reshape(1, C), wa, wx, ba.reshape(1, C), bx.reshape(1, C),
      lam.reshape(1, C), g.reshape(1, C))


def _attn_kernel(q_ref, k_ref, v_ref, o_ref, bias_s, x4_s, acc_s, m_s, l_s, *, S):
    qi = lax.broadcasted_iota(jnp.int32, (2 * BLK, 2 * BLK), 0) % BLK
    ki = lax.broadcasted_iota(jnp.int32, (2 * BLK, 2 * BLK), 1)
    dist = qi + BLK - ki
    bias_s[...] = jnp.where((dist >= 0) & (dist <= BLK), 0.0, NEG_INF).astype(F32)
    lane = lax.broadcasted_iota(jnp.int32, (1, LANES), 1)
    head_a = lane < HEAD_DIM

    Sq = S // 4
    for a, ref in enumerate((q_ref, k_ref, v_ref)):
        for r in range(4):
            x4_s[a, r * Sq:(r + 1) * Sq, :] = ref[0, pl.ds(r, Sq, stride=4), :]

    natural = tuple((lambda rows, ref=ref: ref[0, rows, :]) for ref in (q_ref, k_ref, v_ref))
    mod4 = tuple((lambda rows, a=a: x4_s[a, rows, :]) for a in range(3))

    def tile(p, src, q_rows, k_rows, nk):
        qt = src[0](q_rows)
        kt = src[1](k_rows).astype(BF16)
        vt = src[2](k_rows).astype(BF16)
        q2 = jnp.concatenate([jnp.where(head_a, qt, 0.0), jnp.where(head_a, 0.0, qt)], axis=0)
        s = lax.dot_general(q2.astype(BF16), kt, (((1,), (1,)), ((), ())),
                            preferred_element_type=F32) + bias_s[:, 2 * BLK - nk:2 * BLK]
        yield
        m = jnp.max(s, axis=-1, keepdims=True)
        e = jnp.exp2((s - m).astype(BF16))
        yield
        v_aug = jnp.concatenate([vt, jnp.ones((nk, LANES), BF16)], axis=1)
        r = jnp.dot(e, v_aug, preferred_element_type=F32)
        acc_s[p, q_rows, :] = jnp.where(head_a, r[0:BLK, 0:LANES], r[BLK:, 0:LANES])
        l_s[p, q_rows, :] = jnp.where(head_a, r[0:BLK, LANES:], r[BLK:, LANES:])
        m_s[p, q_rows, :] = jnp.where(head_a, m[0:BLK], m[BLK:])

    def band_tile(p, src, q0, first):
        q0 = pl.multiple_of(q0, BLK)
        q_rows = pl.ds(q0, BLK)
        if first:
            return tile(p, src, q_rows, q_rows, BLK)
        return tile(p, src, q_rows, pl.ds(pl.multiple_of(q0 - BLK, BLK), 2 * BLK), 2 * BLK)

    def groups(n_groups, per_group, emit):
        def body(c, carry):
            tiles = [emit(c, g) for g in range(per_group)]
            for _ in range(3):
                for t in tiles:
                    next(t, None)
            return carry
        if n_groups == 1:
            body(0, 0)
        else:
            lax.fori_loop(0, n_groups, body, 0)

    n_blk = Sq // BLK
    groups(1, 1, lambda c, g: band_tile(0, natural, 0, True))
    groups(3, 5, lambda c, g: band_tile(0, natural, (1 + 5 * c + g) * BLK, False))
    groups(2, 2 * n_blk, lambda c, g: band_tile(1, mod4, (2 * c + g // n_blk) * Sq + (g % n_blk) * BLK,
                                                g % n_blk == 0))

    def dil16(c, g):
        rows = pl.ds((2 * c + g // 4) * Sq + g % 4, BLK, stride=4)
        return tile(2, mod4, rows, rows, BLK)

    groups(2, 8, dil16)

    rows_per = 256

    def combine(c, carry):
        rows = pl.ds(pl.multiple_of(c * rows_per, rows_per), rows_per)
        nat_rows = pl.ds(c // 2 + (c % 2) * (4 * rows_per), rows_per, stride=4)
        m = [m_s[0, nat_rows, :], m_s[1, rows, :], m_s[2, rows, :]]
        mx = jnp.maximum(jnp.maximum(m[0], m[1]), m[2])
        w = [jnp.exp2(mp - mx) for mp in m]
        num = w[0] * acc_s[0, nat_rows, :] + w[1] * acc_s[1, rows, :] + w[2] * acc_s[2, rows, :]
        den = w[0] * l_s[0, nat_rows, :] + w[1] * l_s[1, rows, :] + w[2] * l_s[2, rows, :]
        o_ref[0, nat_rows, :] = (num / den).astype(o_ref.dtype)
        return carry

    lax.fori_loop(0, S // rows_per, combine, 0)


def _attn_call(q, k, v):
    B, S, C = q.shape
    spec = pl.BlockSpec((1, S, LANES), lambda b, p: (b, 0, p))
    return pl.pallas_call(
        functools.partial(_attn_kernel, S=S),
        grid=(B, C // LANES),
        in_specs=[spec, spec, spec],
        out_specs=spec,
        out_shape=jax.ShapeDtypeStruct((B, S, C), F32),
        scratch_shapes=[pltpu.VMEM((2 * BLK, 2 * BLK), F32),
                        pltpu.VMEM((3, S, LANES), F32),
                        pltpu.VMEM((3, S, LANES), F32),
                        pltpu.VMEM((3, S, LANES), F32),
                        pltpu.VMEM((3, S, LANES), F32)],
        compiler_params=pltpu.CompilerParams(
            dimension_semantics=("parallel", "parallel"), vmem_limit_bytes=VMEM_LIMIT),
        name="attn",
    )(q, k, v)


def _outproj_kernel(x_ref, ya_ref, att_ref, gb_ref, mod_ref, natt_ref, npost_ref, w1_ref, w2_ref,
                    o_ref):
    yb = _rms(att_ref[0] * _silu(gb_ref[0]), natt_ref[...])
    mix = (jnp.dot(ya_ref[0], w1_ref[...], preferred_element_type=F32)
           + jnp.dot(yb.astype(BF16), w2_ref[...], preferred_element_type=F32))
    gate = mod_ref[0, :, 2 * D_MODEL:3 * D_MODEL]
    o_ref[0] = x_ref[0] + gate * _rms(mix, npost_ref[...])


def _outproj_call(x, ya, att, gb, mod, natt, npost, w1, w2, ts):
    B, S, D = x.shape
    seq = lambda b, i: (b, i, 0)
    const = lambda b, i: (0, 0)
    half = pl.BlockSpec((1, ts, D_REC), seq)
    return pl.pallas_call(
        _outproj_kernel,
        grid=(B, S // ts),
        in_specs=[pl.BlockSpec((1, ts, D), seq), half, half, half,
                  pl.BlockSpec((1, 1, 3 * D), lambda b, i: (b, 0, 0)),
                  pl.BlockSpec((1, D_ATT), const),
                  pl.BlockSpec((1, D), const),
                  pl.BlockSpec((D_REC, D), const),
                  pl.BlockSpec((D_ATT, D), const)],
        out_specs=pl.BlockSpec((1, ts, D), seq),
        out_shape=jax.ShapeDtypeStruct((B, S, D), F32),
        compiler_params=pltpu.CompilerParams(
            dimension_semantics=("parallel", "parallel"), vmem_limit_bytes=VMEM_LIMIT),
        name="outproj",
    )(x, ya, att, gb, mod.reshape(B, 1, 3 * D), natt.reshape(1, D_ATT), npost.reshape(1, D), w1, w2)


def _block_diag(w):
    nb, n, _ = w.shape
    eye = jnp.eye(nb, dtype=w.dtype)
    return jnp.einsum('hij,hg->higj', w, eye).reshape(nb * n, nb * n)


def kernel(x, c, positions, w_ada, b_ada, norm_pre, norm_post, w_in, conv_w, conv_b, w_rg_a, b_rg_a,
           w_rg_x, b_rg_x, lru_lambda, norm_rec, norm_att, w_out):
    depth = w_in.shape[0]
    inv_freq = ROPE_THETA ** (-jnp.arange(HALF, dtype=F32) / HALF)
    invf = jnp.tile(inv_freq, LANES // HALF).reshape(1, LANES)
    for l in range(depth):
        mod = _mod_call(c, w_ada[l], b_ada[l])
        xa, ga, q, k, v, gb = _inproj_call(x, mod, norm_pre[l], positions, invf,
                                           w_in[l].astype(BF16), ts=256)
        ya = _rec_call(xa, ga, conv_w[l], conv_b[l],
                       _block_diag(w_rg_a[l]).astype(BF16), _block_diag(w_rg_x[l]).astype(BF16),
                       b_rg_a[l], b_rg_x[l], lru_lambda[l], norm_rec[l], T=64)
        att = _attn_call(q, k, v)
        x = _outproj_call(x, ya, att, gb, mod, norm_att[l], norm_post[l],
                          w_out[l, :D_REC].astype(BF16), w_out[l, D_REC:].astype(BF16), ts=256)
    return x
```

```python
import functools

import jax
import jax.numpy as jnp
from jax import lax
from jax.experimental import pallas as pl
from jax.experimental.pallas import tpu as pltpu

F32 = jnp.float32
BF16 = jnp.bfloat16

D_MODEL = 1024
D_REC = 512
D_ATT = 512
N_LRU_BLOCKS = 8
LRU_C = 8.0
CONV_WIDTH = 4
HEAD_DIM = 64
HALF = HEAD_DIM // 2
ROPE_THETA = 10000.0
NORM_EPS = 1e-6
NEG_INF = -1e30
D_IN_PROJ = 2 * D_REC + 4 * D_ATT

LANES = 128
SUBLANES = 8
BLK = 128
DILATIONS = (1, 4, 16)

VMEM_LIMIT = 56 * 1024 * 1024
Q_SCALE = HEAD_DIM ** -0.5 * 1.4426950408889634


def _sigmoid(x):
    return 0.5 * jnp.tanh(0.5 * x) + 0.5


def _silu(x):
    return x * _sigmoid(x)


def _rms(x, g):
    return x * lax.rsqrt(jnp.mean(x * x, axis=-1, keepdims=True) + NORM_EPS) * g


def _mod_kernel(c_ref, w_ref, b_ref, o_ref):
    c = c_ref[...]
    o_ref[...] = jnp.dot(_silu(c), w_ref[...], preferred_element_type=F32,
                         precision=lax.Precision.HIGHEST) + b_ref[...]


def _mod_call(c, w, b):
    B, D = c.shape
    N = w.shape[1]
    tn = 1024
    return pl.pallas_call(
        _mod_kernel,
        grid=(N // tn,),
        in_specs=[pl.BlockSpec((B, D), lambda j: (0, 0)),
                  pl.BlockSpec((D, tn), lambda j: (0, j)),
                  pl.BlockSpec((1, tn), lambda j: (0, j))],
        out_specs=pl.BlockSpec((B, tn), lambda j: (0, j)),
        out_shape=jax.ShapeDtypeStruct((B, N), F32),
        compiler_params=pltpu.CompilerParams(vmem_limit_bytes=VMEM_LIMIT),
        name="mod",
    )(c, w, b.reshape(1, N))


def _inproj_kernel(x_ref, mod_ref, g_ref, pos_ref, invf_ref, w_ref,
                   xa_ref, ga_ref, q_ref, k_ref, v_ref, gb_ref):
    x = x_ref[0]
    shift = mod_ref[0, :, 0:D_MODEL]
    scale = mod_ref[0, :, D_MODEL:2 * D_MODEL]
    h = _rms(x, g_ref[...]) * (1.0 + scale) + shift
    proj = jnp.dot(h.astype(BF16), w_ref[...], preferred_element_type=F32)

    xa_ref[0] = proj[:, 0:D_REC]
    ga_ref[0] = proj[:, D_REC:2 * D_REC]
    o = 2 * D_REC
    v_ref[0] = proj[:, o + 2 * D_ATT:o + 3 * D_ATT]
    gb_ref[0] = proj[:, o + 3 * D_ATT:o + 4 * D_ATT].astype(gb_ref.dtype)

    ang = pos_ref[0].astype(F32) * invf_ref[...]
    cos = jnp.cos(ang)
    sin = jnp.sin(ang)
    lane = lax.broadcasted_iota(jnp.int32, (1, LANES), 1)
    first = (lane % HEAD_DIM) < HALF
    sin_signed = jnp.where(first, -sin, sin)

    def rope(t):
        partner = jnp.where(first, pltpu.roll(t, LANES - HALF, 1), pltpu.roll(t, HALF, 1))
        return t * cos + partner * sin_signed

    for j in range(D_ATT // LANES):
        sl = slice(j * LANES, (j + 1) * LANES)
        q_ref[0, :, sl] = rope(proj[:, o + j * LANES:o + (j + 1) * LANES]) * Q_SCALE
        k_ref[0, :, sl] = rope(proj[:, o + D_ATT + j * LANES:o + D_ATT + (j + 1) * LANES])


def _inproj_call(x, mod, g, pos, invf, w, ts):
    B, S, D = x.shape
    seq = lambda b, i: (b, i, 0)
    const = lambda b, i: (0, 0)
    half = pl.BlockSpec((1, ts, D_REC), seq)
    half_shape = jax.ShapeDtypeStruct((B, S, D_REC), F32)
    return pl.pallas_call(
        _inproj_kernel,
        grid=(B, S // ts),
        in_specs=[pl.BlockSpec((1, ts, D), seq),
                  pl.BlockSpec((1, 1, 3 * D), lambda b, i: (b, 0, 0)),
                  pl.BlockSpec((1, D), const),
                  pl.BlockSpec((1, ts, 1), seq),
                  pl.BlockSpec((1, LANES), const),
                  pl.BlockSpec((D, D_IN_PROJ), const)],
        out_specs=[half] * 6,
        out_shape=[half_shape] * 5 + [jax.ShapeDtypeStruct((B, S, D_ATT), BF16)],
        compiler_params=pltpu.CompilerParams(
            dimension_semantics=("parallel", "parallel"), vmem_limit_bytes=VMEM_LIMIT),
        name="inproj",
    )(x, mod.reshape(B, 1, 3 * D), g.reshape(1, D), pos.reshape(B, S, 1), invf, w)


def _rec_kernel(xa_ref, ga_ref, cw_ref, cb_ref, wa_ref, wx_ref, ba_ref, bx_ref, lam_ref, g_ref,
                o_ref, xs, a_s, u_s, h_s, *, T):
    B = SUBLANES
    tail = (CONV_WIDTH - 1) * B
    n_slab = D_REC // LANES

    @pl.when(pl.program_id(0) == 0)
    def _():
        xs[:, 0:tail, :] = jnp.zeros((n_slab, tail, LANES), F32)
        h_s[...] = jnp.zeros((n_slab, B, LANES), F32)

    for b in range(B):
        for j in range(n_slab):
            xs[j, pl.ds(tail + b, T, stride=B), :] = xa_ref[b, :, j * LANES:(j + 1) * LANES]

    xc = jnp.concatenate(
        [sum(cw_ref[k:k + 1, j * LANES:(j + 1) * LANES] * xs[j, k * B:k * B + T * B, :]
             for k in range(CONV_WIDTH)) for j in range(n_slab)], axis=-1) + cb_ref[...]
    xs[:, 0:tail, :] = xs[:, T * B:T * B + tail, :]

    xcb = xc.astype(BF16)
    r = _sigmoid(jnp.dot(xcb, wa_ref[...], preferred_element_type=F32) + ba_ref[...])
    ig = _sigmoid(jnp.dot(xcb, wx_ref[...], preferred_element_type=F32) + bx_ref[...])
    z = -lam_ref[...]
    softplus = jnp.maximum(z, 0.0) + jnp.log1p(jnp.exp(-jnp.abs(z)))
    log_a = (-LRU_C) * r * softplus
    a = jnp.exp(log_a)
    u = jnp.sqrt(-jnp.tanh(log_a) * (1.0 + a * a)) * (ig * xc)
    for j in range(n_slab):
        a_s[j] = a[:, j * LANES:(j + 1) * LANES]
        u_s[j] = u[:, j * LANES:(j + 1) * LANES]

    def step(t, h):
        rows = pl.ds(pl.multiple_of(t * B, B), B)
        h = a_s[:, rows, :] * h + u_s[:, rows, :]
        u_s[:, rows, :] = h
        return h

    h_s[...] = lax.fori_loop(0, T, step, h_s[...], unroll=8)

    for b in range(B):
        hb = jnp.concatenate([u_s[j, pl.ds(b, T, stride=B), :] for j in range(n_slab)], axis=-1)
        o_ref[b] = _rms(hb * _silu(ga_ref[b]), g_ref[...]).astype(o_ref.dtype)


def _rec_call(xa, ga, cw, cb, wa, wx, ba, bx, lam, g, T):
    B, S, C = xa.shape
    assert B == SUBLANES
    seq = pl.BlockSpec((B, T, C), lambda i: (0, i, 0))
    vec = pl.BlockSpec((1, C), lambda i: (0, 0))
    mat = pl.BlockSpec((C, C), lambda i: (0, 0))
    return pl.pallas_call(
        functools.partial(_rec_kernel, T=T),
        grid=(S // T,),
        in_specs=[seq, seq, pl.BlockSpec((CONV_WIDTH, C), lambda i: (0, 0)), vec, mat, mat,
                  vec, vec, vec, vec],
        out_specs=seq,
        out_shape=jax.ShapeDtypeStruct((B, S, C), BF16),
        scratch_shapes=[pltpu.VMEM((C // LANES, (T + CONV_WIDTH - 1) * B, LANES), F32),
                        pltpu.VMEM((C // LANES, T * B, LANES), F32),
                        pltpu.VMEM((C // LANES, T * B, LANES), F32),
                        pltpu.VMEM((C // LANES, B, LANES), F32)],
        compiler_params=pltpu.CompilerParams(
            dimension_semantics=("arbitrary",), vmem_limit_bytes=VMEM_LIMIT),
        name="rec",
    )(xa, ga, cw, cb.reshape(1, C), wa, wx, ba.reshape(1, C), bx.reshape(1, C),
      lam.reshape(1, C), g.reshape(1, C))


def _attn_kernel(q_ref, k_ref, v_ref, o_ref, bias_s, x4_s, acc_s, m_s, l_s, *, S):
    qi = lax.broadcasted_iota(jnp.int32, (2 * BLK, 2 * BLK), 0) % BLK
    ki = lax.broadcasted_iota(jnp.int32, (2 * BLK, 2 * BLK), 1)
    dist = qi + BLK - ki
    bias_s[...] = jnp.where((dist >= 0) & (dist <= BLK), 0.0, NEG_INF).astype(F32)
    lane = lax.broadcasted_iota(jnp.int32, (1, LANES), 1)
    head_a = lane < HEAD_DIM

    Sq = S // 4
    for a, ref in enumerate((q_ref, k_ref, v_ref)):
        for r in range(4):
            x4_s[a, r * Sq:(r + 1) * Sq, :] = ref[0, pl.ds(r, Sq, stride=4), :]

    natural = tuple((lambda rows, ref=ref: ref[0, rows, :]) for ref in (q_ref, k_ref, v_ref))
    mod4 = tuple((lambda rows, a=a: x4_s[a, rows, :]) for a in range(3))

    def tile(p, src, q_rows, k_rows, nk):
        qt = src[0](q_rows)
        kt = src[1](k_rows).astype(BF16)
        vt = src[2](k_rows).astype(BF16)
        q2 = jnp.concatenate([jnp.where(head_a, qt, 0.0), jnp.where(head_a, 0.0, qt)], axis=0)
        s = lax.dot_general(q2.astype(BF16), kt, (((1,), (1,)), ((), ())),
                            preferred_element_type=F32) + bias_s[:, 2 * BLK - nk:2 * BLK]
        yield
        m = jnp.max(s, axis=-1, keepdims=True)
        e = jnp.exp2((s - m).astype(BF16))
        yield
        v_aug = jnp.concatenate([vt, jnp.ones((nk, LANES), BF16)], axis=1)
        r = jnp.dot(e, v_aug, preferred_element_type=F32)
        acc_s[p, q_rows, :] = jnp.where(head_a, r[0:BLK, 0:LANES], r[BLK:, 0:LANES])
        l_s[p, q_rows, :] = jnp.where(head_a, r[0:BLK, LANES:], r[BLK:, LANES:])
        m_s[p, q_rows, :] = jnp.where(head_a, m[0:BLK], m[BLK:])

    def band_tile(p, src, q0, first):
        q0 = pl.multiple_of(q0, BLK)
        q_rows = pl.ds(q0, BLK)
        if first:
            return tile(p, src, q_rows, q_rows, BLK)
        return tile(p, src, q_rows, pl.ds(pl.multiple_of(q0 - BLK, BLK), 2 * BLK), 2 * BLK)

    def groups(n_groups, per_group, emit):
        def body(c, carry):
            tiles = [emit(c, g) for g in range(per_group)]
            for _ in range(3):
                for t in tiles:
                    next(t, None)
            return carry
        if n_groups == 1:
            body(0, 0)
        else:
            lax.fori_loop(0, n_groups, body, 0)

    n_blk = Sq // BLK
    groups(1, 1, lambda c, g: band_tile(0, natural, 0, True))
    groups(3, 5, lambda c, g: band_tile(0, natural, (1 + 5 * c + g) * BLK, False))
    groups(2, 2 * n_blk, lambda c, g: band_tile(1, mod4, (2 * c + g // n_blk) * Sq + (g % n_blk) * BLK,
                                                g % n_blk == 0))

    def dil16(c, g):
        rows = pl.ds((2 * c + g // 4) * Sq + g % 4, BLK, stride=4)
        return tile(2, mod4, rows, rows, BLK)

    groups(2, 8, dil16)

    rows_per = 256

    def combine(c, carry):
        rows = pl.ds(pl.multiple_of(c * rows_per, rows_per), rows_per)
        nat_rows = pl.ds(c // 2 + (c % 2) * (4 * rows_per), rows_per, stride=4)
        m = [m_s[0, nat_rows, :], m_s[1, rows, :], m_s[2, rows, :]]
        mx = jnp.maximum(jnp.maximum(m[0], m[1]), m[2])
        w = [jnp.exp2(mp - mx) for mp in m]
        num = w[0] * acc_s[0, nat_rows, :] + w[1] * acc_s[1, rows, :] + w[2] * acc_s[2, rows, :]
        den = w[0] * l_s[0, nat_rows, :] + w[1] * l_s[1, rows, :] + w[2] * l_s[2, rows, :]
        o_ref[0, nat_rows, :] = (num / den).astype(o_ref.dtype)
        return carry

    lax.fori_loop(0, S // rows_per, combine, 0)


def _attn_call(q, k, v):
    B, S, C = q.shape
    spec = pl.BlockSpec((1, S, LANES), lambda b, p: (b, 0, p))
    return pl.pallas_call(
        functools.partial(_attn_kernel, S=S),
        grid=(B, C // LANES),
        in_specs=[spec, spec, spec],
        out_specs=spec,
        out_shape=jax.ShapeDtypeStruct((B, S, C), F32),
        scratch_shapes=[pltpu.VMEM((2 * BLK, 2 * BLK), F32),
                        pltpu.VMEM((3, S, LANES), F32),
                        pltpu.VMEM((3, S, LANES), F32),
                        pltpu.VMEM((3, S, LANES), F32),
                        pltpu.VMEM((3, S, LANES), F32)],
        compiler_params=pltpu.CompilerParams(
            dimension_semantics=("parallel", "parallel"), vmem_limit_bytes=VMEM_LIMIT),
        name="attn",
    )(q, k, v)


def _outproj_kernel(x_ref, ya_ref, att_ref, gb_ref, mod_ref, natt_ref, npost_ref, w1_ref, w2_ref,
                    o_ref):
    yb = _rms(att_ref[0] * _silu(gb_ref[0].astype(F32)), natt_ref[...])
    mix = (jnp.dot(ya_ref[0], w1_ref[...], preferred_element_type=F32)
           + jnp.dot(yb.astype(BF16), w2_ref[...], preferred_element_type=F32))
    gate = mod_ref[0, :, 2 * D_MODEL:3 * D_MODEL]
    o_ref[0] = x_ref[0] + gate * _rms(mix, npost_ref[...])


def _outproj_call(x, ya, att, gb, mod, natt, npost, w1, w2, ts):
    B, S, D = x.shape
    seq = lambda b, i: (b, i, 0)
    const = lambda b, i: (0, 0)
    half = pl.BlockSpec((1, ts, D_REC), seq)
    return pl.pallas_call(
        _outproj_kernel,
        grid=(B, S // ts),
        in_specs=[pl.BlockSpec((1, ts, D), seq), half, half, half,
                  pl.BlockSpec((1, 1, 3 * D), lambda b, i: (b, 0, 0)),
                  pl.BlockSpec((1, D_ATT), const),
                  pl.BlockSpec((1, D), const),
                  pl.BlockSpec((D_REC, D), const),
                  pl.BlockSpec((D_ATT, D), const)],
        out_specs=pl.BlockSpec((1, ts, D), seq),
        out_shape=jax.ShapeDtypeStruct((B, S, D), F32),
        compiler_params=pltpu.CompilerParams(
            dimension_semantics=("parallel", "parallel"), vmem_limit_bytes=VMEM_LIMIT),
        name="outproj",
    )(x, ya, att, gb, mod.reshape(B, 1, 3 * D), natt.reshape(1, D_ATT), npost.reshape(1, D), w1, w2)


def _block_diag(w):
    nb, n, _ = w.shape
    eye = jnp.eye(nb, dtype=w.dtype)
    return jnp.einsum('hij,hg->higj', w, eye).reshape(nb * n, nb * n)


def kernel(x, c, positions, w_ada, b_ada, norm_pre, norm_post, w_in, conv_w, conv_b, w_rg_a, b_rg_a,
           w_rg_x, b_rg_x, lru_lambda, norm_rec, norm_att, w_out):
    depth = w_in.shape[0]
    inv_freq = ROPE_THETA ** (-jnp.arange(HALF, dtype=F32) / HALF)
    invf = jnp.tile(inv_freq, LANES // HALF).reshape(1, LANES)
    for l in range(depth):
        mod = _mod_call(c, w_ada[l], b_ada[l])
        xa, ga, q, k, v, gb = _inproj_call(x, mod, norm_pre[l], positions, invf,
                                           w_in[l].astype(BF16), ts=512)
        ya = _rec_call(xa, ga, conv_w[l], conv_b[l],
                       _block_diag(w_rg_a[l]).astype(BF16), _block_diag(w_rg_x[l]).astype(BF16),
                       b_rg_a[l], b_rg_x[l], lru_lambda[l], norm_rec[l], T=64)
        att = _attn_call(q, k, v)
        x = _outproj_call(x, ya, att, gb, mod, norm_att[l], norm_post[l],
                          w_out[l, :D_REC].astype(BF16), w_out[l, D_REC:].astype(BF16), ts=512)
    return x
```

```python
import functools

import jax
import jax.numpy as jnp
from jax import lax
from jax.experimental import pallas as pl
from jax.experimental.pallas import tpu as pltpu

F32 = jnp.float32
BF16 = jnp.bfloat16

D_MODEL = 1024
D_REC = 512
D_ATT = 512
N_LRU_BLOCKS = 8
LRU_C = 8.0
CONV_WIDTH = 4
HEAD_DIM = 64
HALF = HEAD_DIM // 2
ROPE_THETA = 10000.0
NORM_EPS = 1e-6
NEG_INF = -1e30
D_IN_PROJ = 2 * D_REC + 4 * D_ATT

LANES = 128
SUBLANES = 8
BLK = 128
DILATIONS = (1, 4, 16)

VMEM_LIMIT = 56 * 1024 * 1024
Q_SCALE = HEAD_DIM ** -0.5 * 1.4426950408889634


def _sigmoid(x):
    return 0.5 * jnp.tanh(0.5 * x) + 0.5


def _silu(x):
    return x * _sigmoid(x)


def _rms(x, g):
    return x * lax.rsqrt(jnp.mean(x * x, axis=-1, keepdims=True) + NORM_EPS) * g


def _mod_kernel(c_ref, w_ref, b_ref, o_ref):
    c = c_ref[...]
    o_ref[...] = jnp.dot(_silu(c), w_ref[...], preferred_element_type=F32,
                         precision=lax.Precision.HIGHEST) + b_ref[...]


def _mod_call(c, w, b):
    B, D = c.shape
    N = w.shape[1]
    tn = 1024
    return pl.pallas_call(
        _mod_kernel,
        grid=(N // tn,),
        in_specs=[pl.BlockSpec((B, D), lambda j: (0, 0)),
                  pl.BlockSpec((D, tn), lambda j: (0, j)),
                  pl.BlockSpec((1, tn), lambda j: (0, j))],
        out_specs=pl.BlockSpec((B, tn), lambda j: (0, j)),
        out_shape=jax.ShapeDtypeStruct((B, N), F32),
        compiler_params=pltpu.CompilerParams(vmem_limit_bytes=VMEM_LIMIT),
        name="mod",
    )(c, w, b.reshape(1, N))


def _inproj_kernel(x_ref, mod_ref, g_ref, pos_ref, invf_ref, w_ref,
                   xa_ref, ga_ref, q_ref, k_ref, v_ref, gb_ref):
    x = x_ref[0]
    shift = mod_ref[0, :, 0:D_MODEL]
    scale = mod_ref[0, :, D_MODEL:2 * D_MODEL]
    h = _rms(x, g_ref[...]) * (1.0 + scale) + shift
    proj = jnp.dot(h.astype(BF16), w_ref[...], preferred_element_type=F32)

    xa_ref[0] = proj[:, 0:D_REC]
    ga_ref[0] = proj[:, D_REC:2 * D_REC]
    o = 2 * D_REC
    v_ref[0] = proj[:, o + 2 * D_ATT:o + 3 * D_ATT]
    gb_ref[0] = proj[:, o + 3 * D_ATT:o + 4 * D_ATT].astype(gb_ref.dtype)

    ang = pos_ref[0].astype(F32) * invf_ref[...]
    cos = jnp.cos(ang)
    sin = jnp.sin(ang)
    lane = lax.broadcasted_iota(jnp.int32, (1, LANES), 1)
    first = (lane % HEAD_DIM) < HALF
    sin_signed = jnp.where(first, -sin, sin)

    def rope(t):
        partner = jnp.where(first, pltpu.roll(t, LANES - HALF, 1), pltpu.roll(t, HALF, 1))
        return t * cos + partner * sin_signed

    for j in range(D_ATT // LANES):
        sl = slice(j * LANES, (j + 1) * LANES)
        q_ref[0, :, sl] = rope(proj[:, o + j * LANES:o + (j + 1) * LANES]) * Q_SCALE
        k_ref[0, :, sl] = rope(proj[:, o + D_ATT + j * LANES:o + D_ATT + (j + 1) * LANES])


def _inproj_call(x, mod, g, pos, invf, w, ts):
    B, S, D = x.shape
    seq = lambda b, i: (b, i, 0)
    const = lambda b, i: (0, 0)
    half = pl.BlockSpec((1, ts, D_REC), seq)
    half_shape = jax.ShapeDtypeStruct((B, S, D_REC), F32)
    return pl.pallas_call(
        _inproj_kernel,
        grid=(B, S // ts),
        in_specs=[pl.BlockSpec((1, ts, D), seq),
                  pl.BlockSpec((1, 1, 3 * D), lambda b, i: (b, 0, 0)),
                  pl.BlockSpec((1, D), const),
                  pl.BlockSpec((1, ts, 1), seq),
                  pl.BlockSpec((1, LANES), const),
                  pl.BlockSpec((D, D_IN_PROJ), const)],
        out_specs=[half] * 6,
        out_shape=[half_shape] * 5 + [jax.ShapeDtypeStruct((B, S, D_ATT), BF16)],
        compiler_params=pltpu.CompilerParams(
            dimension_semantics=("parallel", "parallel"), vmem_limit_bytes=VMEM_LIMIT),
        name="inproj",
    )(x, mod.reshape(B, 1, 3 * D), g.reshape(1, D), pos.reshape(B, S, 1), invf, w)


def _rec_kernel(xa_ref, ga_ref, cw_ref, cb_ref, wa_ref, wx_ref, ba_ref, bx_ref, lam_ref, g_ref,
                o_ref, xs, a_s, u_s, h_s, *, T):
    B = SUBLANES
    tail = (CONV_WIDTH - 1) * B
    n_slab = D_REC // LANES

    @pl.when(pl.program_id(0) == 0)
    def _():
        xs[:, 0:tail, :] = jnp.zeros((n_slab, tail, LANES), F32)
        h_s[...] = jnp.zeros((n_slab, B, LANES), F32)

    for b in range(B):
        for j in range(n_slab):
            xs[j, pl.ds(tail + b, T, stride=B), :] = xa_ref[b, :, j * LANES:(j + 1) * LANES]

    xc = jnp.concatenate(
        [sum(cw_ref[k:k + 1, j * LANES:(j + 1) * LANES] * xs[j, k * B:k * B + T * B, :]
             for k in range(CONV_WIDTH)) for j in range(n_slab)], axis=-1) + cb_ref[...]
    xs[:, 0:tail, :] = xs[:, T * B:T * B + tail, :]

    xcb = xc.astype(BF16)
    r = _sigmoid(jnp.dot(xcb, wa_ref[...], preferred_element_type=F32) + ba_ref[...])
    ig = _sigmoid(jnp.dot(xcb, wx_ref[...], preferred_element_type=F32) + bx_ref[...])
    z = -lam_ref[...]
    softplus = jnp.maximum(z, 0.0) + jnp.log1p(jnp.exp(-jnp.abs(z)))
    log_a = (-LRU_C) * r * softplus
    a = jnp.exp(log_a)
    u = jnp.sqrt(-jnp.tanh(log_a) * (1.0 + a * a)) * (ig * xc)
    for j in range(n_slab):
        a_s[j] = a[:, j * LANES:(j + 1) * LANES]
        u_s[j] = u[:, j * LANES:(j + 1) * LANES]

    def step(t, h):
        rows = pl.ds(pl.multiple_of(t * B, B), B)
        h = a_s[:, rows, :] * h + u_s[:, rows, :]
        u_s[:, rows, :] = h
        return h

    h_s[...] = lax.fori_loop(0, T, step, h_s[...], unroll=8)

    for b in range(B):
        hb = jnp.concatenate([u_s[j, pl.ds(b, T, stride=B), :] for j in range(n_slab)], axis=-1)
        o_ref[b] = _rms(hb * _silu(ga_ref[b]), g_ref[...]).astype(o_ref.dtype)


def _rec_call(xa, ga, cw, cb, wa, wx, ba, bx, lam, g, T):
    B, S, C = xa.shape
    assert B == SUBLANES
    seq = pl.BlockSpec((B, T, C), lambda i: (0, i, 0))
    vec = pl.BlockSpec((1, C), lambda i: (0, 0))
    mat = pl.BlockSpec((C, C), lambda i: (0, 0))
    return pl.pallas_call(
        functools.partial(_rec_kernel, T=T),
        grid=(S // T,),
        in_specs=[seq, seq, pl.BlockSpec((CONV_WIDTH, C), lambda i: (0, 0)), vec, mat, mat,
                  vec, vec, vec, vec],
        out_specs=seq,
        out_shape=jax.ShapeDtypeStruct((B, S, C), BF16),
        scratch_shapes=[pltpu.VMEM((C // LANES, (T + CONV_WIDTH - 1) * B, LANES), F32),
                        pltpu.VMEM((C // LANES, T * B, LANES), F32),
                        pltpu.VMEM((C // LANES, T * B, LANES), F32),
                        pltpu.VMEM((C // LANES, B, LANES), F32)],
        compiler_params=pltpu.CompilerParams(
            dimension_semantics=("arbitrary",), vmem_limit_bytes=VMEM_LIMIT),
        name="rec",
    )(xa, ga, cw, cb.reshape(1, C), wa, wx, ba.reshape(1, C), bx.reshape(1, C),
      lam.reshape(1, C), g.reshape(1, C))


def _attn_kernel(q_ref, k_ref, v_ref, o_ref, bias_s, x4_s, acc_s, m_s, l_s, *, S):
    qi = lax.broadcasted_iota(jnp.int32, (2 * BLK, 2 * BLK), 0) % BLK
    ki = lax.broadcasted_iota(jnp.int32, (2 * BLK, 2 * BLK), 1)
    dist = qi + BLK - ki
    bias_s[...] = jnp.where((dist >= 0) & (dist <= BLK), 0.0, NEG_INF).astype(F32)
    lane = lax.broadcasted_iota(jnp.int32, (1, LANES), 1)
    head_a = lane < HEAD_DIM

    Sq = S // 4
    for a, ref in enumerate((q_ref, k_ref, v_ref)):
        for r in range(4):
            x4_s[a, r * Sq:(r + 1) * Sq, :] = ref[0, pl.ds(r, Sq, stride=4), :]

    natural = tuple((lambda rows, ref=ref: ref[0, rows, :]) for ref in (q_ref, k_ref, v_ref))
    mod4 = tuple((lambda rows, a=a: x4_s[a, rows, :]) for a in range(3))

    def tile(p, src, q_rows, k_rows, nk):
        qt = src[0](q_rows)
        kt = src[1](k_rows).astype(BF16)
        vt = src[2](k_rows).astype(BF16)
        q2 = jnp.concatenate([jnp.where(head_a, qt, 0.0), jnp.where(head_a, 0.0, qt)], axis=0)
        s = lax.dot_general(q2.astype(BF16), kt, (((1,), (1,)), ((), ())),
                            preferred_element_type=F32) + bias_s[:, 2 * BLK - nk:2 * BLK]
        yield
        m = jnp.max(s, axis=-1, keepdims=True)
        e = jnp.exp2((s - m).astype(BF16))
        yield
        v_aug = jnp.concatenate([vt, jnp.ones((nk, LANES), BF16)], axis=1)
        r = jnp.dot(e, v_aug, preferred_element_type=F32)
        acc_s[p, q_rows, :] = jnp.where(head_a, r[0:BLK, 0:LANES], r[BLK:, 0:LANES])
        l_s[p, q_rows, :] = jnp.where(head_a, r[0:BLK, LANES:], r[BLK:, LANES:])
        m_s[p, q_rows, :] = jnp.where(head_a, m[0:BLK], m[BLK:])

    def band_tile(p, src, q0, first):
        q_rows = pl.ds(q0, BLK)
        if first:
            return tile(p, src, q_rows, q_rows, BLK)
        return tile(p, src, q_rows, pl.ds(q0 - BLK, 2 * BLK), 2 * BLK)

    n_blk = Sq // BLK
    tiles = [band_tile(0, natural, n * BLK, n == 0) for n in range(S // BLK)]
    tiles += [band_tile(1, mod4, r * Sq + n * BLK, n == 0) for r in range(4) for n in range(n_blk)]
    for r in range(4):
        for e in range(4):
            rows = pl.ds(r * Sq + e, BLK, stride=4)
            tiles.append(tile(2, mod4, rows, rows, BLK))

    n_stage = 3
    for k in range(len(tiles) + n_stage - 1):
        for stage in range(n_stage):
            if 0 <= k - stage < len(tiles):
                next(tiles[k - stage], None)

    rows_per = 256
    for c in range(S // rows_per):
        rows = pl.ds(c * rows_per, rows_per)
        nat_rows = pl.ds(c // 2 + (c % 2) * (4 * rows_per), rows_per, stride=4)
        m = [m_s[0, nat_rows, :], m_s[1, rows, :], m_s[2, rows, :]]
        mx = jnp.maximum(jnp.maximum(m[0], m[1]), m[2])
        w = [jnp.exp2(mp - mx) for mp in m]
        num = w[0] * acc_s[0, nat_rows, :] + w[1] * acc_s[1, rows, :] + w[2] * acc_s[2, rows, :]
        den = w[0] * l_s[0, nat_rows, :] + w[1] * l_s[1, rows, :] + w[2] * l_s[2, rows, :]
        o_ref[0, nat_rows, :] = (num / den).astype(o_ref.dtype)


def _attn_call(q, k, v):
    B, S, C = q.shape
    spec = pl.BlockSpec((1, S, LANES), lambda b, p: (b, 0, p))
    return pl.pallas_call(
        functools.partial(_attn_kernel, S=S),
        grid=(B, C // LANES),
        in_specs=[spec, spec, spec],
        out_specs=spec,
        out_shape=jax.ShapeDtypeStruct((B, S, C), F32),
        scratch_shapes=[pltpu.VMEM((2 * BLK, 2 * BLK), F32),
                        pltpu.VMEM((3, S, LANES), F32),
                        pltpu.VMEM((3, S, LANES), F32),
                        pltpu.VMEM((3, S, LANES), F32),
                        pltpu.VMEM((3, S, LANES), F32)],
        compiler_params=pltpu.CompilerParams(
            dimension_semantics=("parallel", "parallel"), vmem_limit_bytes=VMEM_LIMIT),
        name="attn",
    )(q, k, v)


def _outproj_kernel(x_ref, ya_ref, att_ref, gb_ref, mod_ref, natt_ref, npost_ref, w1_ref, w2_ref,
                    o_ref):
    yb = _rms(att_ref[0] * _silu(gb_ref[0].astype(F32)), natt_ref[...])
    mix = (jnp.dot(ya_ref[0], w1_ref[...], preferred_element_type=F32)
           + jnp.dot(yb.astype(BF16), w2_ref[...], preferred_element_type=F32))
    gate = mod_ref[0, :, 2 * D_MODEL:3 * D_MODEL]
    o_ref[0] = x_ref[0] + gate * _rms(mix, npost_ref[...])


def _outproj_call(x, ya, att, gb, mod, natt, npost, w1, w2, ts):
    B, S, D = x.shape
    seq = lambda b, i: (b, i, 0)
    const = lambda b, i: (0, 0)
    half = pl.BlockSpec((1, ts, D_REC), seq)
    return pl.pallas_call(
        _outproj_kernel,
        grid=(B, S // ts),
        in_specs=[pl.BlockSpec((1, ts, D), seq), half, half, half,
                  pl.BlockSpec((1, 1, 3 * D), lambda b, i: (b, 0, 0)),
                  pl.BlockSpec((1, D_ATT), const),
                  pl.BlockSpec((1, D), const),
                  pl.BlockSpec((D_REC, D), const),
                  pl.BlockSpec((D_ATT, D), const)],
        out_specs=pl.BlockSpec((1, ts, D), seq),
        out_shape=jax.ShapeDtypeStruct((B, S, D), F32),
        compiler_params=pltpu.CompilerParams(
            dimension_semantics=("parallel", "parallel"), vmem_limit_bytes=VMEM_LIMIT),
        name="outproj",
    )(x, ya, att, gb, mod.reshape(B, 1, 3 * D), natt.reshape(1, D_ATT), npost.reshape(1, D), w1, w2)


def _block_diag(w):
    nb, n, _ = w.shape
    eye = jnp.eye(nb, dtype=w.dtype)
    return jnp.einsum('hij,hg->higj', w, eye).reshape(nb * n, nb * n)


def kernel(x, c, positions, w_ada, b_ada, norm_pre, norm_post, w_in, conv_w, conv_b, w_rg_a, b_rg_a,
           w_rg_x, b_rg_x, lru_lambda, norm_rec, norm_att, w_out):
    depth = w_in.shape[0]
    inv_freq = ROPE_THETA ** (-jnp.arange(HALF, dtype=F32) / HALF)
    invf = jnp.tile(inv_freq, LANES // HALF).reshape(1, LANES)
    for l in range(depth):
        mod = _mod_call(c, w_ada[l], b_ada[l])
        xa, ga, q, k, v, gb = _inproj_call(x, mod, norm_pre[l], positions, invf,
                                           w_in[l].astype(BF16), ts=512)
        ya = _rec_call(xa, ga, conv_w[l], conv_b[l],
                       _block_diag(w_rg_a[l]).astype(BF16), _block_diag(w_rg_x[l]).astype(BF16),
                       b_rg_a[l], b_rg_x[l], lru_lambda[l], norm_rec[l], T=64)
        att = _attn_call(q, k, v)
        x = _outproj_call(x, ya, att, gb, mod, norm_att[l], norm_post[l],
                          w_out[l, :D_REC].astype(BF16), w_out[l, D_REC:].astype(BF16), ts=512)
    return x
```

```python
import functools

import jax
import jax.numpy as jnp
from jax import lax
from jax.experimental import pallas as pl
from jax.experimental.pallas import tpu as pltpu

F32 = jnp.float32
BF16 = jnp.bfloat16

D_MODEL = 1024
D_REC = 512
D_ATT = 512
N_LRU_BLOCKS = 8
LRU_C = 8.0
CONV_WIDTH = 4
HEAD_DIM = 64
HALF = HEAD_DIM // 2
ROPE_THETA = 10000.0
NORM_EPS = 1e-6
NEG_INF = -1e30
D_IN_PROJ = 2 * D_REC + 4 * D_ATT

LANES = 128
SUBLANES = 8
BLK = 128
DILATIONS = (1, 4, 16)

VMEM_LIMIT = 56 * 1024 * 1024
Q_SCALE = HEAD_DIM ** -0.5 * 1.4426950408889634


def _sigmoid(x):
    return 0.5 * jnp.tanh(0.5 * x) + 0.5


def _silu(x):
    return x * _sigmoid(x)


def _rms(x, g):
    return x * lax.rsqrt(jnp.mean(x * x, axis=-1, keepdims=True) + NORM_EPS) * g


def _mod_kernel(c_ref, w_ref, b_ref, o_ref):
    c = c_ref[...]
    o_ref[...] = jnp.dot(_silu(c), w_ref[...], preferred_element_type=F32,
                         precision=lax.Precision.HIGHEST) + b_ref[...]


def _mod_call(c, w, b):
    B, D = c.shape
    N = w.shape[1]
    tn = 1024
    return pl.pallas_call(
        _mod_kernel,
        grid=(N // tn,),
        in_specs=[pl.BlockSpec((B, D), lambda j: (0, 0)),
                  pl.BlockSpec((D, tn), lambda j: (0, j)),
                  pl.BlockSpec((1, tn), lambda j: (0, j))],
        out_specs=pl.BlockSpec((B, tn), lambda j: (0, j)),
        out_shape=jax.ShapeDtypeStruct((B, N), F32),
        compiler_params=pltpu.CompilerParams(vmem_limit_bytes=VMEM_LIMIT),
        name="mod",
    )(c, w, b.reshape(1, N))


def _inproj_kernel(x_ref, mod_ref, g_ref, pos_ref, invf_ref, w_ref,
                   xa_ref, ga_ref, q_ref, k_ref, v_ref, gb_ref):
    x = x_ref[0]
    shift = mod_ref[0, :, 0:D_MODEL]
    scale = mod_ref[0, :, D_MODEL:2 * D_MODEL]
    h = _rms(x, g_ref[...]) * (1.0 + scale) + shift
    proj = jnp.dot(h.astype(BF16), w_ref[...], preferred_element_type=F32)

    xa_ref[0] = proj[:, 0:D_REC]
    ga_ref[0] = proj[:, D_REC:2 * D_REC]
    o = 2 * D_REC
    v_ref[0] = proj[:, o + 2 * D_ATT:o + 3 * D_ATT]
    gb_ref[0] = proj[:, o + 3 * D_ATT:o + 4 * D_ATT].astype(gb_ref.dtype)

    ang = pos_ref[0].astype(F32) * invf_ref[...]
    cos = jnp.cos(ang)
    sin = jnp.sin(ang)
    lane = lax.broadcasted_iota(jnp.int32, (1, LANES), 1)
    first = (lane % HEAD_DIM) < HALF
    sin_signed = jnp.where(first, -sin, sin)

    def rope(t):
        partner = jnp.where(first, pltpu.roll(t, LANES - HALF, 1), pltpu.roll(t, HALF, 1))
        return t * cos + partner * sin_signed

    for j in range(D_ATT // LANES):
        sl = slice(j * LANES, (j + 1) * LANES)
        q_ref[0, :, sl] = rope(proj[:, o + j * LANES:o + (j + 1) * LANES]) * Q_SCALE
        k_ref[0, :, sl] = rope(proj[:, o + D_ATT + j * LANES:o + D_ATT + (j + 1) * LANES])


def _inproj_call(x, mod, g, pos, invf, w, ts):
    B, S, D = x.shape
    seq = lambda b, i: (b, i, 0)
    const = lambda b, i: (0, 0)
    half = pl.BlockSpec((1, ts, D_REC), seq)
    half_shape = jax.ShapeDtypeStruct((B, S, D_REC), F32)
    return pl.pallas_call(
        _inproj_kernel,
        grid=(B, S // ts),
        in_specs=[pl.BlockSpec((1, ts, D), seq),
                  pl.BlockSpec((1, 1, 3 * D), lambda b, i: (b, 0, 0)),
                  pl.BlockSpec((1, D), const),
                  pl.BlockSpec((1, ts, 1), seq),
                  pl.BlockSpec((1, LANES), const),
                  pl.BlockSpec((D, D_IN_PROJ), const)],
        out_specs=[half] * 6,
        out_shape=[half_shape] * 5 + [jax.ShapeDtypeStruct((B, S, D_ATT), BF16)],
        compiler_params=pltpu.CompilerParams(
            dimension_semantics=("parallel", "parallel"), vmem_limit_bytes=VMEM_LIMIT),
        name="inproj",
    )(x, mod.reshape(B, 1, 3 * D), g.reshape(1, D), pos.reshape(B, S, 1), invf, w)


def _rec_stages(xa_ref, ga_ref, cw_ref, cb_ref, wa_ref, wx_ref, ba_ref, bx_ref, lam_ref, g_ref,
                o_ref, xs, a_s, u_s, h_s, *, T):
    B = SUBLANES
    tail = (CONV_WIDTH - 1) * B
    n_slab = D_REC // LANES
    slab = lambda j: slice(j * LANES, (j + 1) * LANES)

    for b in range(B):
        for j in range(n_slab):
            xs[j, pl.ds(tail + b, T, stride=B), :] = xa_ref[b, :, slab(j)]
        yield

    z = -lam_ref[...]
    log_a_rate = (-LRU_C) * (jnp.maximum(z, 0.0) + jnp.log1p(jnp.exp(-jnp.abs(z))))
    chunk = 16 * B
    for r0 in range(0, T * B, chunk):
        xc = jnp.concatenate(
            [sum(cw_ref[k:k + 1, slab(j)] * xs[j, k * B + r0:k * B + r0 + chunk, :]
                 for k in range(CONV_WIDTH)) for j in range(n_slab)], axis=-1) + cb_ref[...]
        xcb = xc.astype(BF16)
        r = _sigmoid(jnp.dot(xcb, wa_ref[...], preferred_element_type=F32) + ba_ref[...])
        ig = _sigmoid(jnp.dot(xcb, wx_ref[...], preferred_element_type=F32) + bx_ref[...])
        log_a = r * log_a_rate
        a = jnp.exp(log_a)
        u = jnp.sqrt(-jnp.tanh(log_a) * (1.0 + a * a)) * (ig * xc)
        for j in range(n_slab):
            a_s[j, r0:r0 + chunk, :] = a[:, slab(j)]
            u_s[j, r0:r0 + chunk, :] = u[:, slab(j)]
        yield
    xs[:, 0:tail, :] = xs[:, T * B:T * B + tail, :]

    h = h_s[...]
    for t in range(T):
        rows = pl.ds(t * B, B)
        h = a_s[:, rows, :] * h + u_s[:, rows, :]
        u_s[:, rows, :] = h
        if t % 8 == 7:
            yield
    h_s[...] = h

    for b in range(B):
        hb = jnp.concatenate([u_s[j, pl.ds(b, T, stride=B), :] for j in range(n_slab)], axis=-1)
        o_ref[b] = _rms(hb * _silu(ga_ref[b].astype(F32)), g_ref[...]).astype(o_ref.dtype)
        yield


def _attn_stages(q_ref, k_ref, v_ref, o_ref, bias_s, x4_s, acc_s, m_s, l_s, *, S):
    qi = lax.broadcasted_iota(jnp.int32, (2 * BLK, 2 * BLK), 0) % BLK
    ki = lax.broadcasted_iota(jnp.int32, (2 * BLK, 2 * BLK), 1)
    dist = qi + BLK - ki
    bias_s[...] = jnp.where((dist >= 0) & (dist <= BLK), 0.0, NEG_INF).astype(F32)
    lane = lax.broadcasted_iota(jnp.int32, (1, LANES), 1)
    head_a = lane < HEAD_DIM

    Sq = S // 4
    for a, ref in enumerate((q_ref, k_ref, v_ref)):
        for r in range(4):
            x4_s[a, r * Sq:(r + 1) * Sq, :] = ref[0, pl.ds(r, Sq, stride=4), :]
    yield

    natural = tuple((lambda rows, ref=ref: ref[0, rows, :]) for ref in (q_ref, k_ref, v_ref))
    mod4 = tuple((lambda rows, a=a: x4_s[a, rows, :]) for a in range(3))

    def tile(p, src, q_rows, k_rows, nk):
        qt = src[0](q_rows)
        kt = src[1](k_rows).astype(BF16)
        vt = src[2](k_rows).astype(BF16)
        q2 = jnp.concatenate([jnp.where(head_a, qt, 0.0), jnp.where(head_a, 0.0, qt)], axis=0)
        s = lax.dot_general(q2.astype(BF16), kt, (((1,), (1,)), ((), ())),
                            preferred_element_type=F32) + bias_s[:, 2 * BLK - nk:2 * BLK]
        yield
        m = jnp.max(s, axis=-1, keepdims=True)
        e = jnp.exp2((s - m).astype(BF16))
        yield
        v_aug = jnp.concatenate([vt, jnp.ones((nk, LANES), BF16)], axis=1)
        r = jnp.dot(e, v_aug, preferred_element_type=F32)
        acc_s[p, q_rows, :] = jnp.where(head_a, r[0:BLK, 0:LANES], r[BLK:, 0:LANES])
        l_s[p, q_rows, :] = jnp.where(head_a, r[0:BLK, LANES:], r[BLK:, LANES:])
        m_s[p, q_rows, :] = jnp.where(head_a, m[0:BLK], m[BLK:])

    def band_tile(p, src, q0, first):
        q_rows = pl.ds(q0, BLK)
        if first:
            return tile(p, src, q_rows, q_rows, BLK)
        return tile(p, src, q_rows, pl.ds(q0 - BLK, 2 * BLK), 2 * BLK)

    n_blk = Sq // BLK
    tiles = [band_tile(0, natural, n * BLK, n == 0) for n in range(S // BLK)]
    tiles += [band_tile(1, mod4, r * Sq + n * BLK, n == 0) for r in range(4) for n in range(n_blk)]
    for r in range(4):
        for e in range(4):
            rows = pl.ds(r * Sq + e, BLK, stride=4)
            tiles.append(tile(2, mod4, rows, rows, BLK))

    n_stage = 3
    for k in range(len(tiles) + n_stage - 1):
        for stage in range(n_stage):
            if 0 <= k - stage < len(tiles):
                next(tiles[k - stage], None)
        yield

    rows_per = 256
    for c in range(S // rows_per):
        rows = pl.ds(c * rows_per, rows_per)
        nat_rows = pl.ds(c // 2 + (c % 2) * (4 * rows_per), rows_per, stride=4)
        m = [m_s[0, nat_rows, :], m_s[1, rows, :], m_s[2, rows, :]]
        mx = jnp.maximum(jnp.maximum(m[0], m[1]), m[2])
        w = [jnp.exp2(mp - mx) for mp in m]
        num = w[0] * acc_s[0, nat_rows, :] + w[1] * acc_s[1, rows, :] + w[2] * acc_s[2, rows, :]
        den = w[0] * l_s[0, nat_rows, :] + w[1] * l_s[1, rows, :] + w[2] * l_s[2, rows, :]
        o_ref[0, nat_rows, :] = (num / den).astype(o_ref.dtype)
        yield


def _mixers_kernel(q_ref, k_ref, v_ref, xa_ref, ga_ref, cw_ref, cb_ref, wa_ref, wx_ref, ba_ref,
                   bx_ref, lam_ref, g_ref, att_ref, ya_ref,
                   bias_s, x4_s, acc_s, m_s, l_s, xs, a_s, u_s, h_s, *, S, T):
    @pl.when(pl.program_id(0) == 0)
    def _():
        xs[:, 0:(CONV_WIDTH - 1) * SUBLANES, :] = jnp.zeros(
            (xs.shape[0], (CONV_WIDTH - 1) * SUBLANES, LANES), F32)
        h_s[...] = jnp.zeros(h_s.shape, F32)

    attn = _attn_stages(q_ref, k_ref, v_ref, att_ref, bias_s, x4_s, acc_s, m_s, l_s, S=S)
    rec = _rec_stages(xa_ref, ga_ref, cw_ref, cb_ref, wa_ref, wx_ref, ba_ref, bx_ref, lam_ref,
                      g_ref, ya_ref, xs, a_s, u_s, h_s, T=T)
    step = 0
    for _ in attn:
        if step % 2 == 1:
            next(rec, None)
        step += 1
    for _ in rec:
        pass


def _mixers_call(q, k, v, xa, ga, cw, cb, wa, wx, ba, bx, lam, g):
    B, S, C = q.shape
    assert B == SUBLANES and xa.shape == (B, S, D_REC)
    n_pair = C // LANES
    n_step = B * n_pair
    T = S // n_step
    head = pl.BlockSpec((1, S, LANES), lambda s: (s // n_pair, 0, s % n_pair))
    seq = pl.BlockSpec((B, T, D_REC), lambda s: (0, s, 0))
    vec = pl.BlockSpec((1, D_REC), lambda s: (0, 0))
    mat = pl.BlockSpec((D_REC, D_REC), lambda s: (0, 0))
    n_slab = D_REC // LANES
    return pl.pallas_call(
        functools.partial(_mixers_kernel, S=S, T=T),
        grid=(n_step,),
        in_specs=[head, head, head, seq, seq,
                  pl.BlockSpec((CONV_WIDTH, D_REC), lambda s: (0, 0)), vec, mat, mat,
                  vec, vec, vec, vec],
        out_specs=[head, seq],
        out_shape=[jax.ShapeDtypeStruct((B, S, C), F32), jax.ShapeDtypeStruct((B, S, D_REC), BF16)],
        scratch_shapes=[pltpu.VMEM((2 * BLK, 2 * BLK), F32),
                        pltpu.VMEM((3, S, LANES), F32),
                        pltpu.VMEM((3, S, LANES), F32),
                        pltpu.VMEM((3, S, LANES), F32),
                        pltpu.VMEM((3, S, LANES), F32),
                        pltpu.VMEM((n_slab, (T + CONV_WIDTH - 1) * B, LANES), F32),
                        pltpu.VMEM((n_slab, T * B, LANES), F32),
                        pltpu.VMEM((n_slab, T * B, LANES), F32),
                        pltpu.VMEM((n_slab, B, LANES), F32)],
        compiler_params=pltpu.CompilerParams(
            dimension_semantics=("arbitrary",), vmem_limit_bytes=VMEM_LIMIT),
        name="mixers",
    )(q, k, v, xa, ga, cw, cb.reshape(1, D_REC), wa, wx, ba.reshape(1, D_REC),
      bx.reshape(1, D_REC), lam.reshape(1, D_REC), g.reshape(1, D_REC))


def _outproj_kernel(x_ref, ya_ref, att_ref, gb_ref, mod_ref, natt_ref, npost_ref, w1_ref, w2_ref,
                    o_ref):
    yb = _rms(att_ref[0] * _silu(gb_ref[0].astype(F32)), natt_ref[...])
    mix = (jnp.dot(ya_ref[0], w1_ref[...], preferred_element_type=F32)
           + jnp.dot(yb.astype(BF16), w2_ref[...], preferred_element_type=F32))
    gate = mod_ref[0, :, 2 * D_MODEL:3 * D_MODEL]
    o_ref[0] = x_ref[0] + gate * _rms(mix, npost_ref[...])


def _outproj_call(x, ya, att, gb, mod, natt, npost, w1, w2, ts):
    B, S, D = x.shape
    seq = lambda b, i: (b, i, 0)
    const = lambda b, i: (0, 0)
    half = pl.BlockSpec((1, ts, D_REC), seq)
    return pl.pallas_call(
        _outproj_kernel,
        grid=(B, S // ts),
        in_specs=[pl.BlockSpec((1, ts, D), seq), half, half, half,
                  pl.BlockSpec((1, 1, 3 * D), lambda b, i: (b, 0, 0)),
                  pl.BlockSpec((1, D_ATT), const),
                  pl.BlockSpec((1, D), const),
                  pl.BlockSpec((D_REC, D), const),
                  pl.BlockSpec((D_ATT, D), const)],
        out_specs=pl.BlockSpec((1, ts, D), seq),
        out_shape=jax.ShapeDtypeStruct((B, S, D), F32),
        compiler_params=pltpu.CompilerParams(
            dimension_semantics=("parallel", "parallel"), vmem_limit_bytes=VMEM_LIMIT),
        name="outproj",
    )(x, ya, att, gb, mod.reshape(B, 1, 3 * D), natt.reshape(1, D_ATT), npost.reshape(1, D), w1, w2)


def _block_diag(w):
    nb, n, _ = w.shape
    eye = jnp.eye(nb, dtype=w.dtype)
    return jnp.einsum('hij,hg->higj', w, eye).reshape(nb * n, nb * n)


def kernel(x, c, positions, w_ada, b_ada, norm_pre, norm_post, w_in, conv_w, conv_b, w_rg_a, b_rg_a,
           w_rg_x, b_rg_x, lru_lambda, norm_rec, norm_att, w_out):
    depth = w_in.shape[0]
    inv_freq = ROPE_THETA ** (-jnp.arange(HALF, dtype=F32) / HALF)
    invf = jnp.tile(inv_freq, LANES // HALF).reshape(1, LANES)
    for l in range(depth):
        mod = _mod_call(c, w_ada[l], b_ada[l])
        xa, ga, q, k, v, gb = _inproj_call(x, mod, norm_pre[l], positions, invf,
                                           w_in[l].astype(BF16), ts=512)
        att, ya = _mixers_call(q, k, v, xa, ga, conv_w[l], conv_b[l],
                               _block_diag(w_rg_a[l]).astype(BF16),
                               _block_diag(w_rg_x[l]).astype(BF16),
                               b_rg_a[l], b_rg_x[l], lru_lambda[l], norm_rec[l])
        x = _outproj_call(x, ya, att, gb, mod, norm_att[l], norm_post[l],
                          w_out[l, :D_REC].astype(BF16), w_out[l, D_REC:].astype(BF16), ts=512)
    return x
```

```python
import functools

import jax
import jax.numpy as jnp
from jax import lax
from jax.experimental import pallas as pl
from jax.experimental.pallas import tpu as pltpu

F32 = jnp.float32
BF16 = jnp.bfloat16

D_MODEL = 1024
D_REC = 512
D_ATT = 512
N_LRU_BLOCKS = 8
LRU_C = 8.0
CONV_WIDTH = 4
HEAD_DIM = 64
HALF = HEAD_DIM // 2
ROPE_THETA = 10000.0
NORM_EPS = 1e-6
NEG_INF = -1e30
D_IN_PROJ = 2 * D_REC + 4 * D_ATT

LANES = 128
SUBLANES = 8
BLK = 128
DILATIONS = (1, 4, 16)

VMEM_LIMIT = 56 * 1024 * 1024
Q_SCALE = HEAD_DIM ** -0.5 * 1.4426950408889634


def _sigmoid(x):
    return 0.5 * jnp.tanh(0.5 * x) + 0.5


def _silu(x):
    h = 0.5 * x
    return h + h * jnp.tanh(h)


def _rms(x, g):
    return x * lax.rsqrt(jnp.mean(x * x, axis=-1, keepdims=True) + NORM_EPS) * g


def _mod_kernel(c_ref, w_ref, b_ref, o_ref):
    c = c_ref[...]
    o_ref[...] = jnp.dot(_silu(c), w_ref[...], preferred_element_type=F32,
                         precision=lax.Precision.HIGHEST) + b_ref[...]


def _mod_call(c, w, b):
    B, D = c.shape
    N = w.shape[1]
    tn = 1024
    return pl.pallas_call(
        _mod_kernel,
        grid=(N // tn,),
        in_specs=[pl.BlockSpec((B, D), lambda j: (0, 0)),
                  pl.BlockSpec((D, tn), lambda j: (0, j)),
                  pl.BlockSpec((1, tn), lambda j: (0, j))],
        out_specs=pl.BlockSpec((B, tn), lambda j: (0, j)),
        out_shape=jax.ShapeDtypeStruct((B, N), F32),
        compiler_params=pltpu.CompilerParams(vmem_limit_bytes=VMEM_LIMIT),
        name="mod",
    )(c, w, b.reshape(1, N))


def _skewed(stages, n_stage):
    for k in range(len(stages) + n_stage - 1):
        for stage in range(n_stage):
            if 0 <= k - stage < len(stages):
                next(stages[k - stage], None)


SUB_ROWS = 256


def _inproj_rows(rows, x_ref, mod_ref, g_ref, pos_ref, invf_ref, w_ref,
                 xa_ref, ga_ref, q_ref, k_ref, v_ref, gb_ref):
    shift = mod_ref[0, :, 0:D_MODEL]
    scale = mod_ref[0, :, D_MODEL:2 * D_MODEL]
    h = (_rms(x_ref[0, rows, :], g_ref[...]) * (1.0 + scale) + shift).astype(BF16)
    yield
    proj = jnp.dot(h, w_ref[...], preferred_element_type=F32)
    yield
    xa_ref[0, rows, :] = proj[:, 0:D_REC]
    ga_ref[0, rows, :] = proj[:, D_REC:2 * D_REC]
    o = 2 * D_REC
    v_ref[0, rows, :] = proj[:, o + 2 * D_ATT:o + 3 * D_ATT]
    gb_ref[0, rows, :] = proj[:, o + 3 * D_ATT:o + 4 * D_ATT].astype(gb_ref.dtype)

    ang = pos_ref[0, rows, :].astype(F32) * invf_ref[...]
    cos = jnp.cos(ang)
    sin = jnp.sin(ang)
    lane = lax.broadcasted_iota(jnp.int32, (1, LANES), 1)
    first = (lane % HEAD_DIM) < HALF
    sin_signed = jnp.where(first, -sin, sin)

    def rope(t):
        partner = jnp.where(first, pltpu.roll(t, LANES - HALF, 1), pltpu.roll(t, HALF, 1))
        return t * cos + partner * sin_signed

    for j in range(D_ATT // LANES):
        sl = slice(j * LANES, (j + 1) * LANES)
        q_ref[0, rows, sl] = rope(proj[:, o + j * LANES:o + (j + 1) * LANES]) * Q_SCALE
        k_ref[0, rows, sl] = rope(proj[:, o + D_ATT + j * LANES:o + D_ATT + (j + 1) * LANES])


def _inproj_kernel(x_ref, *refs):
    ts = x_ref.shape[1]
    _skewed([_inproj_rows(pl.ds(r0, SUB_ROWS), x_ref, *refs) for r0 in range(0, ts, SUB_ROWS)], 3)


def _inproj_call(x, mod, g, pos, invf, w, ts):
    B, S, D = x.shape
    seq = lambda b, i: (b, i, 0)
    const = lambda b, i: (0, 0)
    half = pl.BlockSpec((1, ts, D_REC), seq)
    half_shape = jax.ShapeDtypeStruct((B, S, D_REC), F32)
    return pl.pallas_call(
        _inproj_kernel,
        grid=(B, S // ts),
        in_specs=[pl.BlockSpec((1, ts, D), seq),
                  pl.BlockSpec((1, 1, 3 * D), lambda b, i: (b, 0, 0)),
                  pl.BlockSpec((1, D), const),
                  pl.BlockSpec((1, ts, 1), seq),
                  pl.BlockSpec((1, LANES), const),
                  pl.BlockSpec((D, D_IN_PROJ), const)],
        out_specs=[half] * 6,
        out_shape=[half_shape] * 5 + [jax.ShapeDtypeStruct((B, S, D_ATT), BF16)],
        compiler_params=pltpu.CompilerParams(
            dimension_semantics=("parallel", "parallel"), vmem_limit_bytes=VMEM_LIMIT),
        name="inproj",
    )(x, mod.reshape(B, 1, 3 * D), g.reshape(1, D), pos.reshape(B, S, 1), invf, w)


def _rec_stages(xa_ref, ga_ref, cw_ref, cb_ref, wa_ref, wx_ref, ba_ref, bx_ref, lam_ref, g_ref,
                o_ref, xs, a_s, u_s, h_s, *, T):
    B = SUBLANES
    tail = (CONV_WIDTH - 1) * B
    n_slab = D_REC // LANES
    slab = lambda j: slice(j * LANES, (j + 1) * LANES)

    for b in range(B):
        for j in range(n_slab):
            xs[j, pl.ds(tail + b, T, stride=B), :] = xa_ref[b, :, slab(j)]
        yield

    z = -lam_ref[...]
    half_rate = (-0.5 * LRU_C) * (jnp.maximum(z, 0.0) + jnp.log1p(jnp.exp(-jnp.abs(z))))
    half_ba = 0.5 * ba_ref[...]
    half_bx = 0.5 * bx_ref[...]
    chunk = 16 * B
    for r0 in range(0, T * B, chunk):
        xc = jnp.concatenate(
            [sum(cw_ref[k:k + 1, slab(j)] * xs[j, k * B + r0:k * B + r0 + chunk, :]
                 for k in range(CONV_WIDTH)) for j in range(n_slab)], axis=-1) + cb_ref[...]
        xcb = xc.astype(BF16)
        ta = jnp.tanh(jnp.dot(xcb, wa_ref[...], preferred_element_type=F32) + half_ba)
        ig = 0.5 * jnp.tanh(jnp.dot(xcb, wx_ref[...], preferred_element_type=F32) + half_bx) + 0.5
        log_a = ta * half_rate + half_rate
        a = jnp.exp(log_a)
        u = jnp.sqrt(-jnp.tanh(log_a) * (1.0 + a * a)) * (ig * xc)
        for j in range(n_slab):
            a_s[j, r0:r0 + chunk, :] = a[:, slab(j)]
            u_s[j, r0:r0 + chunk, :] = u[:, slab(j)]
        yield
    xs[:, 0:tail, :] = xs[:, T * B:T * B + tail, :]

    h = h_s[...]
    for t in range(T):
        rows = pl.ds(t * B, B)
        h = a_s[:, rows, :] * h + u_s[:, rows, :]
        u_s[:, rows, :] = h
        if t % 8 == 7:
            yield
    h_s[...] = h

    for b in range(B):
        hb = jnp.concatenate([u_s[j, pl.ds(b, T, stride=B), :] for j in range(n_slab)], axis=-1)
        o_ref[b] = _rms(hb * _silu(ga_ref[b].astype(F32)), g_ref[...]).astype(o_ref.dtype)
        yield


def _attn_stages(q_ref, k_ref, v_ref, o_ref, bias_s, x4_s, acc_s, m_s, l_s, *, S):
    qi = lax.broadcasted_iota(jnp.int32, (2 * BLK, 2 * BLK), 0) % BLK
    ki = lax.broadcasted_iota(jnp.int32, (2 * BLK, 2 * BLK), 1)
    dist = qi + BLK - ki
    bias_s[...] = jnp.where((dist >= 0) & (dist <= BLK), 0.0, NEG_INF).astype(F32)
    lane = lax.broadcasted_iota(jnp.int32, (1, LANES), 1)
    head_a = lane < HEAD_DIM

    Sq = S // 4

    def reorder(a, ref, r):
        x4_s[a, r * Sq:(r + 1) * Sq, :] = ref[0, pl.ds(r, Sq, stride=4), :]

    reorders = [functools.partial(reorder, a, ref, r)
                for a, ref in enumerate((q_ref, k_ref, v_ref)) for r in range(4)]

    natural = tuple((lambda rows, ref=ref: ref[0, rows, :]) for ref in (q_ref, k_ref, v_ref))
    mod4 = tuple((lambda rows, a=a: x4_s[a, rows, :]) for a in range(3))

    def tile(p, src, q_rows, k_rows, nk):
        qt = src[0](q_rows)
        kt = src[1](k_rows).astype(BF16)
        vt = src[2](k_rows).astype(BF16)
        q2 = jnp.concatenate([jnp.where(head_a, qt, 0.0), jnp.where(head_a, 0.0, qt)], axis=0)
        s = lax.dot_general(q2.astype(BF16), kt, (((1,), (1,)), ((), ())),
                            preferred_element_type=F32) + bias_s[:, 2 * BLK - nk:2 * BLK]
        yield
        m = jnp.max(s, axis=-1, keepdims=True)
        e = jnp.exp2((s - m).astype(BF16))
        yield
        v_aug = jnp.concatenate([vt, jnp.ones((nk, LANES), BF16)], axis=1)
        r = jnp.dot(e, v_aug, preferred_element_type=F32)
        acc_s[p, q_rows, :] = jnp.where(head_a, r[0:BLK, 0:LANES], r[BLK:, 0:LANES])
        l_s[p, q_rows, :] = jnp.where(head_a, r[0:BLK, LANES:], r[BLK:, LANES:])
        m_s[p, q_rows, :] = jnp.where(head_a, m[0:BLK], m[BLK:])

    def band_tile(p, src, q0, first):
        q_rows = pl.ds(q0, BLK)
        if first:
            return tile(p, src, q_rows, q_rows, BLK)
        return tile(p, src, q_rows, pl.ds(q0 - BLK, 2 * BLK), 2 * BLK)

    n_blk = Sq // BLK
    tiles = [band_tile(0, natural, n * BLK, n == 0) for n in range(S // BLK)]
    tiles += [band_tile(1, mod4, r * Sq + n * BLK, n == 0) for r in range(4) for n in range(n_blk)]
    for r in range(4):
        for e in range(4):
            rows = pl.ds(r * Sq + e, BLK, stride=4)
            tiles.append(tile(2, mod4, rows, rows, BLK))

    rows_per = 256

    def combine(c):
        rows = pl.ds(c * rows_per, rows_per)
        nat_rows = pl.ds(c // 2 + (c % 2) * (4 * rows_per), rows_per, stride=4)
        m = [m_s[0, nat_rows, :], m_s[1, rows, :], m_s[2, rows, :]]
        mx = jnp.maximum(jnp.maximum(m[0], m[1]), m[2])
        w = [jnp.exp2(mp - mx) for mp in m]
        num = w[0] * acc_s[0, nat_rows, :] + w[1] * acc_s[1, rows, :] + w[2] * acc_s[2, rows, :]
        den = w[0] * l_s[0, nat_rows, :] + w[1] * l_s[1, rows, :] + w[2] * l_s[2, rows, :]
        o_ref[0, nat_rows, :] = (num / den).astype(o_ref.dtype)

    n_stage = 3
    chunks_per_res = Sq // rows_per
    first_d16 = len(tiles) - 16
    ready = {}
    for c in range(S // rows_per):
        last_tile = first_d16 + 4 * (c // chunks_per_res) + 3
        ready.setdefault(last_tile + n_stage + c % chunks_per_res, []).append(c)

    n_step = len(tiles) + n_stage - 1
    assert len(reorders) <= S // BLK - 2
    for k in range(max(n_step, max(ready) + 1)):
        if k < len(reorders):
            reorders[k]()
        for stage in range(n_stage):
            if 0 <= k - stage < len(tiles):
                next(tiles[k - stage], None)
        for c in ready.get(k, []):
            combine(c)
        yield


def _mixers_kernel(q_ref, k_ref, v_ref, xa_ref, ga_ref, cw_ref, cb_ref, wa_ref, wx_ref, ba_ref,
                   bx_ref, lam_ref, g_ref, att_ref, ya_ref,
                   bias_s, x4_s, acc_s, m_s, l_s, xs, a_s, u_s, h_s, *, S, T):
    @pl.when(pl.program_id(0) == 0)
    def _():
        xs[:, 0:(CONV_WIDTH - 1) * SUBLANES, :] = jnp.zeros(
            (xs.shape[0], (CONV_WIDTH - 1) * SUBLANES, LANES), F32)
        h_s[...] = jnp.zeros(h_s.shape, F32)

    attn = _attn_stages(q_ref, k_ref, v_ref, att_ref, bias_s, x4_s, acc_s, m_s, l_s, S=S)
    rec = _rec_stages(xa_ref, ga_ref, cw_ref, cb_ref, wa_ref, wx_ref, ba_ref, bx_ref, lam_ref,
                      g_ref, ya_ref, xs, a_s, u_s, h_s, T=T)
    step = 0
    for _ in attn:
        if step % 2 == 1:
            next(rec, None)
        step += 1
    for _ in rec:
        pass


def _mixers_call(q, k, v, xa, ga, cw, cb, wa, wx, ba, bx, lam, g):
    B, S, C = q.shape
    assert B == SUBLANES and xa.shape == (B, S, D_REC)
    n_pair = C // LANES
    n_step = B * n_pair
    T = S // n_step
    head = pl.BlockSpec((1, S, LANES), lambda s: (s // n_pair, 0, s % n_pair))
    seq = pl.BlockSpec((B, T, D_REC), lambda s: (0, s, 0))
    vec = pl.BlockSpec((1, D_REC), lambda s: (0, 0))
    mat = pl.BlockSpec((D_REC, D_REC), lambda s: (0, 0))
    n_slab = D_REC // LANES
    return pl.pallas_call(
        functools.partial(_mixers_kernel, S=S, T=T),
        grid=(n_step,),
        in_specs=[head, head, head, seq, seq,
                  pl.BlockSpec((CONV_WIDTH, D_REC), lambda s: (0, 0)), vec, mat, mat,
                  vec, vec, vec, vec],
        out_specs=[head, seq],
        out_shape=[jax.ShapeDtypeStruct((B, S, C), F32), jax.ShapeDtypeStruct((B, S, D_REC), BF16)],
        scratch_shapes=[pltpu.VMEM((2 * BLK, 2 * BLK), F32),
                        pltpu.VMEM((3, S, LANES), F32),
                        pltpu.VMEM((3, S, LANES), F32),
                        pltpu.VMEM((3, S, LANES), F32),
                        pltpu.VMEM((3, S, LANES), F32),
                        pltpu.VMEM((n_slab, (T + CONV_WIDTH - 1) * B, LANES), F32),
                        pltpu.VMEM((n_slab, T * B, LANES), F32),
                        pltpu.VMEM((n_slab, T * B, LANES), F32),
                        pltpu.VMEM((n_slab, B, LANES), F32)],
        compiler_params=pltpu.CompilerParams(
            dimension_semantics=("arbitrary",), vmem_limit_bytes=VMEM_LIMIT),
        name="mixers",
    )(q, k, v, xa, ga, cw, cb.reshape(1, D_REC), wa, wx, ba.reshape(1, D_REC),
      bx.reshape(1, D_REC), lam.reshape(1, D_REC), g.reshape(1, D_REC))


def _outproj_rows(rows, x_ref, ya_ref, att_ref, gb_ref, mod_ref, natt_ref, npost_ref, w1_ref,
                  w2_ref, o_ref):
    ya = ya_ref[0, rows, :]
    yb = _rms(att_ref[0, rows, :] * _silu(gb_ref[0, rows, :].astype(F32)), natt_ref[...]).astype(BF16)
    yield
    mix = (jnp.dot(ya, w1_ref[...], preferred_element_type=F32)
           + jnp.dot(yb, w2_ref[...], preferred_element_type=F32))
    yield
    gate = mod_ref[0, :, 2 * D_MODEL:3 * D_MODEL]
    o_ref[0, rows, :] = x_ref[0, rows, :] + gate * _rms(mix, npost_ref[...])


def _outproj_kernel(x_ref, *refs):
    ts = x_ref.shape[1]
    _skewed([_outproj_rows(pl.ds(r0, SUB_ROWS), x_ref, *refs) for r0 in range(0, ts, SUB_ROWS)], 3)


def _outproj_call(x, ya, att, gb, mod, natt, npost, w1, w2, ts):
    B, S, D = x.shape
    seq = lambda b, i: (b, i, 0)
    const = lambda b, i: (0, 0)
    half = pl.BlockSpec((1, ts, D_REC), seq)
    return pl.pallas_call(
        _outproj_kernel,
        grid=(B, S // ts),
        in_specs=[pl.BlockSpec((1, ts, D), seq), half, half, half,
                  pl.BlockSpec((1, 1, 3 * D), lambda b, i: (b, 0, 0)),
                  pl.BlockSpec((1, D_ATT), const),
                  pl.BlockSpec((1, D), const),
                  pl.BlockSpec((D_REC, D), const),
                  pl.BlockSpec((D_ATT, D), const)],
        out_specs=pl.BlockSpec((1, ts, D), seq),
        out_shape=jax.ShapeDtypeStruct((B, S, D), F32),
        compiler_params=pltpu.CompilerParams(
            dimension_semantics=("parallel", "parallel"), vmem_limit_bytes=VMEM_LIMIT),
        name="outproj",
    )(x, ya, att, gb, mod.reshape(B, 1, 3 * D), natt.reshape(1, D_ATT), npost.reshape(1, D), w1, w2)


def _block_diag(w):
    nb, n, _ = w.shape
    eye = jnp.eye(nb, dtype=w.dtype)
    return jnp.einsum('hij,hg->higj', w, eye).reshape(nb * n, nb * n)


def kernel(x, c, positions, w_ada, b_ada, norm_pre, norm_post, w_in, conv_w, conv_b, w_rg_a, b_rg_a,
           w_rg_x, b_rg_x, lru_lambda, norm_rec, norm_att, w_out):
    depth = w_in.shape[0]
    inv_freq = ROPE_THETA ** (-jnp.arange(HALF, dtype=F32) / HALF)
    invf = jnp.tile(inv_freq, LANES // HALF).reshape(1, LANES)
    for l in range(depth):
        mod = _mod_call(c, w_ada[l], b_ada[l])
        xa, ga, q, k, v, gb = _inproj_call(x, mod, norm_pre[l], positions, invf,
                                           w_in[l].astype(BF16), ts=512)
        att, ya = _mixers_call(q, k, v, xa, ga, conv_w[l], conv_b[l],
                               (0.5 * _block_diag(w_rg_a[l])).astype(BF16),
                               (0.5 * _block_diag(w_rg_x[l])).astype(BF16),
                               b_rg_a[l], b_rg_x[l], lru_lambda[l], norm_rec[l])
        x = _outproj_call(x, ya, att, gb, mod, norm_att[l], norm_post[l],
                          w_out[l, :D_REC].astype(BF16), w_out[l, D_REC:].astype(BF16), ts=512)
    return x
```

```python
import functools

import jax
import jax.numpy as jnp
from jax import lax
from jax.experimental import pallas as pl
from jax.experimental.pallas import tpu as pltpu

F32 = jnp.float32
BF16 = jnp.bfloat16

D_MODEL = 1024
D_REC = 512
D_ATT = 512
N_LRU_BLOCKS = 8
LRU_C = 8.0
CONV_WIDTH = 4
HEAD_DIM = 64
HALF = HEAD_DIM // 2
ROPE_THETA = 10000.0
NORM_EPS = 1e-6
NEG_INF = -1e30
D_IN_PROJ = 2 * D_REC + 4 * D_ATT

LANES = 128
SUBLANES = 8
BLK = 128
DILATIONS = (1, 4, 16)

VMEM_LIMIT = 56 * 1024 * 1024
Q_SCALE = HEAD_DIM ** -0.5 * 1.4426950408889634


def _sigmoid(x):
    return 0.5 * jnp.tanh(0.5 * x) + 0.5


def _silu(x):
    h = 0.5 * x
    return h + h * jnp.tanh(h)


def _rms(x, g):
    return x * lax.rsqrt(jnp.mean(x * x, axis=-1, keepdims=True) + NORM_EPS) * g


def _mod_kernel(c_ref, w_ref, b_ref, o_ref):
    c = c_ref[...]
    o_ref[...] = jnp.dot(_silu(c), w_ref[...], preferred_element_type=F32,
                         precision=lax.Precision.HIGHEST) + b_ref[...]


def _mod_call(c, w, b):
    B, D = c.shape
    N = w.shape[1]
    tn = 1024
    return pl.pallas_call(
        _mod_kernel,
        grid=(N // tn,),
        in_specs=[pl.BlockSpec((B, D), lambda j: (0, 0)),
                  pl.BlockSpec((D, tn), lambda j: (0, j)),
                  pl.BlockSpec((1, tn), lambda j: (0, j))],
        out_specs=pl.BlockSpec((B, tn), lambda j: (0, j)),
        out_shape=jax.ShapeDtypeStruct((B, N), F32),
        compiler_params=pltpu.CompilerParams(vmem_limit_bytes=VMEM_LIMIT),
        name="mod",
    )(c, w, b.reshape(1, N))


def _skewed(stages, n_stage):
    for k in range(len(stages) + n_stage - 1):
        for stage in range(n_stage):
            if 0 <= k - stage < len(stages):
                next(stages[k - stage], None)


SUB_ROWS = 256


def _inproj_rows(rows, x_ref, mod_ref, g_ref, pos_ref, invf_ref, w_ref,
                 xa_ref, ga_ref, q_ref, k_ref, v_ref, gb_ref):
    shift = mod_ref[0, :, 0:D_MODEL]
    scale = mod_ref[0, :, D_MODEL:2 * D_MODEL]
    h = (_rms(x_ref[0, rows, :], g_ref[...]) * (1.0 + scale) + shift).astype(BF16)
    yield
    proj = jnp.dot(h, w_ref[...], preferred_element_type=F32)
    yield
    xa_ref[0, rows, :] = proj[:, 0:D_REC]
    ga_ref[0, rows, :] = proj[:, D_REC:2 * D_REC]
    o = 2 * D_REC
    v_ref[0, rows, :] = proj[:, o + 2 * D_ATT:o + 3 * D_ATT]
    gb_ref[0, rows, :] = proj[:, o + 3 * D_ATT:o + 4 * D_ATT].astype(gb_ref.dtype)

    ang = pos_ref[0, rows, :].astype(F32) * invf_ref[...]
    cos = jnp.cos(ang)
    sin = jnp.sin(ang)
    lane = lax.broadcasted_iota(jnp.int32, (1, LANES), 1)
    first = (lane % HEAD_DIM) < HALF
    sin_signed = jnp.where(first, -sin, sin)

    def rope(t):
        partner = jnp.where(first, pltpu.roll(t, LANES - HALF, 1), pltpu.roll(t, HALF, 1))
        return t * cos + partner * sin_signed

    for j in range(D_ATT // LANES):
        sl = slice(j * LANES, (j + 1) * LANES)
        q_ref[0, rows, sl] = rope(proj[:, o + j * LANES:o + (j + 1) * LANES]) * Q_SCALE
        k_ref[0, rows, sl] = rope(proj[:, o + D_ATT + j * LANES:o + D_ATT + (j + 1) * LANES])


def _inproj_kernel(x_ref, *refs):
    ts = x_ref.shape[1]
    _skewed([_inproj_rows(pl.ds(r0, SUB_ROWS), x_ref, *refs) for r0 in range(0, ts, SUB_ROWS)], 3)


def _inproj_call(x, mod, g, pos, invf, w, ts):
    B, S, D = x.shape
    seq = lambda b, i: (b, i, 0)
    const = lambda b, i: (0, 0)
    half = pl.BlockSpec((1, ts, D_REC), seq)
    half_shape = jax.ShapeDtypeStruct((B, S, D_REC), F32)
    return pl.pallas_call(
        _inproj_kernel,
        grid=(B, S // ts),
        in_specs=[pl.BlockSpec((1, ts, D), seq),
                  pl.BlockSpec((1, 1, 3 * D), lambda b, i: (b, 0, 0)),
                  pl.BlockSpec((1, D), const),
                  pl.BlockSpec((1, ts, 1), seq),
                  pl.BlockSpec((1, LANES), const),
                  pl.BlockSpec((D, D_IN_PROJ), const)],
        out_specs=[half] * 6,
        out_shape=[half_shape] * 5 + [jax.ShapeDtypeStruct((B, S, D_ATT), BF16)],
        compiler_params=pltpu.CompilerParams(
            dimension_semantics=("parallel", "parallel"), vmem_limit_bytes=VMEM_LIMIT),
        name="inproj",
    )(x, mod.reshape(B, 1, 3 * D), g.reshape(1, D), pos.reshape(B, S, 1), invf, w)


def _rec_stages(xa_ref, ga_ref, cw_ref, cb_ref, wa_ref, wx_ref, ba_ref, bx_ref, lam_ref, g_ref,
                o_ref, xs, a_s, u_s, h_s, *, T):
    B = SUBLANES
    tail = (CONV_WIDTH - 1) * B
    n_slab = D_REC // LANES
    slab = lambda j: slice(j * LANES, (j + 1) * LANES)

    for b in range(B):
        for j in range(n_slab):
            xs[j, pl.ds(tail + b, T, stride=B), :] = xa_ref[b, :, slab(j)]
        yield

    z = -lam_ref[...]
    half_rate = (-0.5 * LRU_C) * (jnp.maximum(z, 0.0) + jnp.log1p(jnp.exp(-jnp.abs(z))))
    half_ba = 0.5 * ba_ref[...]
    half_bx = 0.5 * bx_ref[...]
    chunk = 16 * B
    for r0 in range(0, T * B, chunk):
        xc = jnp.concatenate(
            [sum(cw_ref[k:k + 1, slab(j)] * xs[j, k * B + r0:k * B + r0 + chunk, :]
                 for k in range(CONV_WIDTH)) for j in range(n_slab)], axis=-1) + cb_ref[...]
        xcb = xc.astype(BF16)
        ta = jnp.tanh(jnp.dot(xcb, wa_ref[...], preferred_element_type=F32) + half_ba)
        ig = 0.5 * jnp.tanh(jnp.dot(xcb, wx_ref[...], preferred_element_type=F32) + half_bx) + 0.5
        log_a = ta * half_rate + half_rate
        a = jnp.exp(log_a)
        u = jnp.sqrt(-jnp.tanh(log_a) * (1.0 + a * a)) * (ig * xc)
        for j in range(n_slab):
            a_s[j, r0:r0 + chunk, :] = a[:, slab(j)]
            u_s[j, r0:r0 + chunk, :] = u[:, slab(j)]
        yield
    xs[:, 0:tail, :] = xs[:, T * B:T * B + tail, :]

    h = h_s[...]
    for t in range(T):
        rows = pl.ds(t * B, B)
        h = a_s[:, rows, :] * h + u_s[:, rows, :]
        u_s[:, rows, :] = h
        if t % 8 == 7:
            yield
    h_s[...] = h

    for b in range(B):
        hb = jnp.concatenate([u_s[j, pl.ds(b, T, stride=B), :] for j in range(n_slab)], axis=-1)
        o_ref[b] = _rms(hb * _silu(ga_ref[b].astype(F32)), g_ref[...]).astype(o_ref.dtype)
        yield


def _attn_stages(q_ref, k_ref, v_ref, o_ref, bias_s, x4_s, acc_s, m_s, l_s, *, S):
    qi = lax.broadcasted_iota(jnp.int32, (2 * BLK, 2 * BLK), 0) % BLK
    ki = lax.broadcasted_iota(jnp.int32, (2 * BLK, 2 * BLK), 1)
    dist = qi + BLK - ki
    bias_s[...] = jnp.where((dist >= 0) & (dist <= BLK), 0.0, NEG_INF).astype(bias_s.dtype)
    lane = lax.broadcasted_iota(jnp.int32, (1, LANES), 1)
    head_a = lane < HEAD_DIM

    Sq = S // 4

    def reorder(a, ref, r):
        x4_s[a, r * Sq:(r + 1) * Sq, :] = ref[0, pl.ds(r, Sq, stride=4), :]

    reorders = [functools.partial(reorder, a, ref, r)
                for a, ref in enumerate((q_ref, k_ref, v_ref)) for r in range(4)]

    natural = tuple((lambda rows, ref=ref: ref[0, rows, :]) for ref in (q_ref, k_ref, v_ref))
    mod4 = tuple((lambda rows, a=a: x4_s[a, rows, :]) for a in range(3))

    def tile(p, src, q_rows, k_rows, nk):
        qt = src[0](q_rows)
        kt = src[1](k_rows).astype(BF16)
        vt = src[2](k_rows).astype(BF16)
        q2 = jnp.concatenate([jnp.where(head_a, qt, 0.0), jnp.where(head_a, 0.0, qt)], axis=0)
        s = lax.dot_general(q2.astype(BF16), kt, (((1,), (1,)), ((), ())),
                            preferred_element_type=F32)
        yield
        sb = s.astype(BF16) + bias_s[:, 2 * BLK - nk:2 * BLK]
        m = jnp.max(sb, axis=-1, keepdims=True)
        e = jnp.exp2(sb - m)
        m = m.astype(F32)
        yield
        v_aug = jnp.concatenate([vt, jnp.ones((nk, LANES), BF16)], axis=1)
        r = jnp.dot(e, v_aug, preferred_element_type=F32)
        acc_s[p, q_rows, :] = jnp.where(head_a, r[0:BLK, 0:LANES], r[BLK:, 0:LANES])
        l_s[p, q_rows, :] = jnp.where(head_a, r[0:BLK, LANES:], r[BLK:, LANES:])
        m_s[p, q_rows, :] = jnp.where(head_a, m[0:BLK], m[BLK:])

    def band_tile(p, src, q0, first):
        q_rows = pl.ds(q0, BLK)
        if first:
            return tile(p, src, q_rows, q_rows, BLK)
        return tile(p, src, q_rows, pl.ds(q0 - BLK, 2 * BLK), 2 * BLK)

    n_blk = Sq // BLK
    tiles = [band_tile(0, natural, n * BLK, n == 0) for n in range(S // BLK)]
    tiles += [band_tile(1, mod4, r * Sq + n * BLK, n == 0) for r in range(4) for n in range(n_blk)]
    for r in range(4):
        for e in range(4):
            rows = pl.ds(r * Sq + e, BLK, stride=4)
            tiles.append(tile(2, mod4, rows, rows, BLK))

    rows_per = 256

    def combine(c):
        rows = pl.ds(c * rows_per, rows_per)
        nat_rows = pl.ds(c // 2 + (c % 2) * (4 * rows_per), rows_per, stride=4)
        m = [m_s[0, nat_rows, :], m_s[1, rows, :], m_s[2, rows, :]]
        mx = jnp.maximum(jnp.maximum(m[0], m[1]), m[2])
        w = [jnp.exp2(mp - mx) for mp in m]
        num = w[0] * acc_s[0, nat_rows, :] + w[1] * acc_s[1, rows, :] + w[2] * acc_s[2, rows, :]
        den = w[0] * l_s[0, nat_rows, :] + w[1] * l_s[1, rows, :] + w[2] * l_s[2, rows, :]
        o_ref[0, nat_rows, :] = (num / den).astype(o_ref.dtype)

    n_stage = 3
    chunks_per_res = Sq // rows_per
    first_d16 = len(tiles) - 16
    ready = {}
    for c in range(S // rows_per):
        last_tile = first_d16 + 4 * (c // chunks_per_res) + 3
        ready.setdefault(last_tile + n_stage + c % chunks_per_res, []).append(c)

    n_step = len(tiles) + n_stage - 1
    assert len(reorders) <= S // BLK - 2
    for k in range(max(n_step, max(ready) + 1)):
        if k < len(reorders):
            reorders[k]()
        for stage in range(n_stage):
            if 0 <= k - stage < len(tiles):
                next(tiles[k - stage], None)
        for c in ready.get(k, []):
            combine(c)
        yield


def _mixers_kernel(q_ref, k_ref, v_ref, xa_ref, ga_ref, cw_ref, cb_ref, wa_ref, wx_ref, ba_ref,
                   bx_ref, lam_ref, g_ref, att_ref, ya_ref,
                   bias_s, x4_s, acc_s, m_s, l_s, xs, a_s, u_s, h_s, *, S, T):
    @pl.when(pl.program_id(0) == 0)
    def _():
        xs[:, 0:(CONV_WIDTH - 1) * SUBLANES, :] = jnp.zeros(
            (xs.shape[0], (CONV_WIDTH - 1) * SUBLANES, LANES), F32)
        h_s[...] = jnp.zeros(h_s.shape, F32)

    attn = _attn_stages(q_ref, k_ref, v_ref, att_ref, bias_s, x4_s, acc_s, m_s, l_s, S=S)
    rec = _rec_stages(xa_ref, ga_ref, cw_ref, cb_ref, wa_ref, wx_ref, ba_ref, bx_ref, lam_ref,
                      g_ref, ya_ref, xs, a_s, u_s, h_s, T=T)
    step = 0
    for _ in attn:
        if step % 2 == 1:
            next(rec, None)
        step += 1
    for _ in rec:
        pass


def _mixers_call(q, k, v, xa, ga, cw, cb, wa, wx, ba, bx, lam, g):
    B, S, C = q.shape
    assert B == SUBLANES and xa.shape == (B, S, D_REC)
    n_pair = C // LANES
    n_step = B * n_pair
    T = S // n_step
    head = pl.BlockSpec((1, S, LANES), lambda s: (s // n_pair, 0, s % n_pair))
    seq = pl.BlockSpec((B, T, D_REC), lambda s: (0, s, 0))
    vec = pl.BlockSpec((1, D_REC), lambda s: (0, 0))
    mat = pl.BlockSpec((D_REC, D_REC), lambda s: (0, 0))
    n_slab = D_REC // LANES
    return pl.pallas_call(
        functools.partial(_mixers_kernel, S=S, T=T),
        grid=(n_step,),
        in_specs=[head, head, head, seq, seq,
                  pl.BlockSpec((CONV_WIDTH, D_REC), lambda s: (0, 0)), vec, mat, mat,
                  vec, vec, vec, vec],
        out_specs=[head, seq],
        out_shape=[jax.ShapeDtypeStruct((B, S, C), F32), jax.ShapeDtypeStruct((B, S, D_REC), BF16)],
        scratch_shapes=[pltpu.VMEM((2 * BLK, 2 * BLK), BF16),
                        pltpu.VMEM((3, S, LANES), F32),
                        pltpu.VMEM((3, S, LANES), F32),
                        pltpu.VMEM((3, S, LANES), F32),
                        pltpu.VMEM((3, S, LANES), F32),
                        pltpu.VMEM((n_slab, (T + CONV_WIDTH - 1) * B, LANES), F32),
                        pltpu.VMEM((n_slab, T * B, LANES), F32),
                        pltpu.VMEM((n_slab, T * B, LANES), F32),
                        pltpu.VMEM((n_slab, B, LANES), F32)],
        compiler_params=pltpu.CompilerParams(
            dimension_semantics=("arbitrary",), vmem_limit_bytes=VMEM_LIMIT),
        name="mixers",
    )(q, k, v, xa, ga, cw, cb.reshape(1, D_REC), wa, wx, ba.reshape(1, D_REC),
      bx.reshape(1, D_REC), lam.reshape(1, D_REC), g.reshape(1, D_REC))


def _outproj_rows(rows, x_ref, ya_ref, att_ref, gb_ref, mod_ref, natt_ref, npost_ref, w1_ref,
                  w2_ref, o_ref):
    ya = ya_ref[0, rows, :]
    yb = _rms(att_ref[0, rows, :] * _silu(gb_ref[0, rows, :].astype(F32)), natt_ref[...]).astype(BF16)
    yield
    mix = (jnp.dot(ya, w1_ref[...], preferred_element_type=F32)
           + jnp.dot(yb, w2_ref[...], preferred_element_type=F32))
    yield
    gate = mod_ref[0, :, 2 * D_MODEL:3 * D_MODEL]
    o_ref[0, rows, :] = x_ref[0, rows, :] + gate * _rms(mix, npost_ref[...])


def _outproj_kernel(x_ref, *refs):
    ts = x_ref.shape[1]
    _skewed([_outproj_rows(pl.ds(r0, SUB_ROWS), x_ref, *refs) for r0 in range(0, ts, SUB_ROWS)], 3)


def _outproj_call(x, ya, att, gb, mod, natt, npost, w1, w2, ts):
    B, S, D = x.shape
    seq = lambda b, i: (b, i, 0)
    const = lambda b, i: (0, 0)
    half = pl.BlockSpec((1, ts, D_REC), seq)
    return pl.pallas_call(
        _outproj_kernel,
        grid=(B, S // ts),
        in_specs=[pl.BlockSpec((1, ts, D), seq), half, half, half,
                  pl.BlockSpec((1, 1, 3 * D), lambda b, i: (b, 0, 0)),
                  pl.BlockSpec((1, D_ATT), const),
                  pl.BlockSpec((1, D), const),
                  pl.BlockSpec((D_REC, D), const),
                  pl.BlockSpec((D_ATT, D), const)],
        out_specs=pl.BlockSpec((1, ts, D), seq),
        out_shape=jax.ShapeDtypeStruct((B, S, D), F32),
        compiler_params=pltpu.CompilerParams(
            dimension_semantics=("parallel", "parallel"), vmem_limit_bytes=VMEM_LIMIT),
        name="outproj",
    )(x, ya, att, gb, mod.reshape(B, 1, 3 * D), natt.reshape(1, D_ATT), npost.reshape(1, D), w1, w2)


def _block_diag(w):
    nb, n, _ = w.shape
    eye = jnp.eye(nb, dtype=w.dtype)
    return jnp.einsum('hij,hg->higj', w, eye).reshape(nb * n, nb * n)


def kernel(x, c, positions, w_ada, b_ada, norm_pre, norm_post, w_in, conv_w, conv_b, w_rg_a, b_rg_a,
           w_rg_x, b_rg_x, lru_lambda, norm_rec, norm_att, w_out):
    depth = w_in.shape[0]
    inv_freq = ROPE_THETA ** (-jnp.arange(HALF, dtype=F32) / HALF)
    invf = jnp.tile(inv_freq, LANES // HALF).reshape(1, LANES)
    for l in range(depth):
        mod = _mod_call(c, w_ada[l], b_ada[l])
        xa, ga, q, k, v, gb = _inproj_call(x, mod, norm_pre[l], positions, invf,
                                           w_in[l].astype(BF16), ts=512)
        att, ya = _mixers_call(q, k, v, xa, ga, conv_w[l], conv_b[l],
                               (0.5 * _block_diag(w_rg_a[l])).astype(BF16),
                               (0.5 * _block_diag(w_rg_x[l])).astype(BF16),
                               b_rg_a[l], b_rg_x[l], lru_lambda[l], norm_rec[l])
        x = _outproj_call(x, ya, att, gb, mod, norm_att[l], norm_post[l],
                          w_out[l, :D_REC].astype(BF16), w_out[l, D_REC:].astype(BF16), ts=512)
    return x
```

```python
import functools

import jax
import jax.numpy as jnp
from jax import lax
from jax.experimental import pallas as pl
from jax.experimental.pallas import tpu as pltpu

F32 = jnp.float32
BF16 = jnp.bfloat16

D_MODEL = 1024
D_REC = 512
D_ATT = 512
N_LRU_BLOCKS = 8
LRU_C = 8.0
CONV_WIDTH = 4
HEAD_DIM = 64
HALF = HEAD_DIM // 2
ROPE_THETA = 10000.0
NORM_EPS = 1e-6
NEG_INF = -1e30
D_IN_PROJ = 2 * D_REC + 4 * D_ATT

LANES = 128
SUBLANES = 8
BLK = 128
DILATIONS = (1, 4, 16)
SKEW = 1

VMEM_LIMIT = 56 * 1024 * 1024
Q_SCALE = HEAD_DIM ** -0.5 * 1.4426950408889634


def _sigmoid(x):
    return 0.5 * jnp.tanh(0.5 * x) + 0.5


def _silu(x):
    h = 0.5 * x
    return h + h * jnp.tanh(h)


def _rms(x, g):
    return x * lax.rsqrt(jnp.mean(x * x, axis=-1, keepdims=True) + NORM_EPS) * g


def _mod_kernel(c_ref, w_ref, b_ref, o_ref):
    c = c_ref[...]
    o_ref[...] = jnp.dot(_silu(c), w_ref[...], preferred_element_type=F32,
                         precision=lax.Precision.HIGHEST) + b_ref[...]


def _mod_call(c, w, b):
    B, D = c.shape
    N = w.shape[1]
    tn = 1024
    return pl.pallas_call(
        _mod_kernel,
        grid=(N // tn,),
        in_specs=[pl.BlockSpec((B, D), lambda j: (0, 0)),
                  pl.BlockSpec((D, tn), lambda j: (0, j)),
                  pl.BlockSpec((1, tn), lambda j: (0, j))],
        out_specs=pl.BlockSpec((B, tn), lambda j: (0, j)),
        out_shape=jax.ShapeDtypeStruct((B, N), F32),
        compiler_params=pltpu.CompilerParams(vmem_limit_bytes=VMEM_LIMIT),
        name="mod",
    )(c, w, b.reshape(1, N))


def _skewed(stages, n_stage):
    for k in range(len(stages) + n_stage - 1):
        for stage in range(n_stage):
            if 0 <= k - stage < len(stages):
                next(stages[k - stage], None)


SUB_ROWS = 256


def _inproj_rows(rows, x_ref, mod_ref, g_ref, pos_ref, invf_ref, w_ref,
                 xa_ref, ga_ref, q_ref, k_ref, v_ref, gb_ref):
    shift = mod_ref[0, :, 0:D_MODEL]
    scale = mod_ref[0, :, D_MODEL:2 * D_MODEL]
    h = (_rms(x_ref[0, rows, :], g_ref[...]) * (1.0 + scale) + shift).astype(BF16)
    yield
    proj = jnp.dot(h, w_ref[...], preferred_element_type=F32)
    yield
    xa_ref[0, rows, :] = proj[:, 0:D_REC]
    ga_ref[0, rows, :] = proj[:, D_REC:2 * D_REC]
    o = 2 * D_REC
    v_ref[0, rows, :] = proj[:, o + 2 * D_ATT:o + 3 * D_ATT]
    gb_ref[0, rows, :] = proj[:, o + 3 * D_ATT:o + 4 * D_ATT].astype(gb_ref.dtype)

    ang = pos_ref[0, rows, :].astype(F32) * invf_ref[...]
    cos = jnp.cos(ang)
    sin = jnp.sin(ang)
    lane = lax.broadcasted_iota(jnp.int32, (1, LANES), 1)
    first = (lane % HEAD_DIM) < HALF
    sin_signed = jnp.where(first, -sin, sin)

    def rope(t):
        partner = jnp.where(first, pltpu.roll(t, LANES - HALF, 1), pltpu.roll(t, HALF, 1))
        return t * cos + partner * sin_signed

    for j in range(D_ATT // LANES):
        sl = slice(j * LANES, (j + 1) * LANES)
        q_ref[0, rows, sl] = rope(proj[:, o + j * LANES:o + (j + 1) * LANES]) * Q_SCALE
        k_ref[0, rows, sl] = rope(proj[:, o + D_ATT + j * LANES:o + D_ATT + (j + 1) * LANES])


def _inproj_kernel(x_ref, *refs):
    ts = x_ref.shape[1]
    _skewed([_inproj_rows(pl.ds(r0, SUB_ROWS), x_ref, *refs) for r0 in range(0, ts, SUB_ROWS)], 3)


def _inproj_call(x, mod, g, pos, invf, w, ts):
    B, S, D = x.shape
    seq = lambda b, i: (b, i, 0)
    const = lambda b, i: (0, 0)
    half = pl.BlockSpec((1, ts, D_REC), seq)
    half_shape = jax.ShapeDtypeStruct((B, S, D_REC), F32)
    return pl.pallas_call(
        _inproj_kernel,
        grid=(B, S // ts),
        in_specs=[pl.BlockSpec((1, ts, D), seq),
                  pl.BlockSpec((1, 1, 3 * D), lambda b, i: (b, 0, 0)),
                  pl.BlockSpec((1, D), const),
                  pl.BlockSpec((1, ts, 1), seq),
                  pl.BlockSpec((1, LANES), const),
                  pl.BlockSpec((D, D_IN_PROJ), const)],
        out_specs=[half] * 6,
        out_shape=[half_shape] * 5 + [jax.ShapeDtypeStruct((B, S, D_ATT), BF16)],
        compiler_params=pltpu.CompilerParams(
            dimension_semantics=("parallel", "parallel"), vmem_limit_bytes=VMEM_LIMIT),
        name="inproj",
    )(x, mod.reshape(B, 1, 3 * D), g.reshape(1, D), pos.reshape(B, S, 1), invf, w)


def _rec_stages(xa_ref, ga_ref, cw_ref, cb_ref, wg_ref, ba_ref, bx_ref, lam_ref, g_ref,
                o_ref, xs, a_s, u_s, h_s, *, T):
    B = SUBLANES
    tail = (CONV_WIDTH - 1) * B
    n_slab = D_REC // LANES
    slab = lambda j: slice(j * LANES, (j + 1) * LANES)

    for b in range(B):
        for j in range(n_slab):
            xs[j, pl.ds(tail + b, T, stride=B), :] = xa_ref[b, :, slab(j)]
        yield

    z = -lam_ref[...]
    half_rate = (-0.5 * LRU_C) * (jnp.maximum(z, 0.0) + jnp.log1p(jnp.exp(-jnp.abs(z))))
    half_ba = 0.5 * ba_ref[...]
    half_bx = 0.5 * bx_ref[...]
    chunk = 16 * B
    for r0 in range(0, T * B, chunk):
        for j in range(n_slab):
            xc = sum(cw_ref[k:k + 1, slab(j)] * xs[j, k * B + r0:k * B + r0 + chunk, :]
                     for k in range(CONV_WIDTH)) + cb_ref[:, slab(j)]
            pre = jnp.dot(xc.astype(BF16), wg_ref[j], preferred_element_type=F32)
            ta = jnp.tanh(pre[:, 0:LANES] + half_ba[:, slab(j)])
            ig = 0.5 * jnp.tanh(pre[:, LANES:] + half_bx[:, slab(j)]) + 0.5
            log_a = ta * half_rate[:, slab(j)] + half_rate[:, slab(j)]
            a = jnp.exp(log_a)
            a_s[j, r0:r0 + chunk, :] = a
            u_s[j, r0:r0 + chunk, :] = jnp.sqrt(-jnp.tanh(log_a) * (1.0 + a * a)) * (ig * xc)
        yield
    xs[:, 0:tail, :] = xs[:, T * B:T * B + tail, :]

    h = h_s[...]
    for t in range(T):
        rows = pl.ds(t * B, B)
        h = a_s[:, rows, :] * h + u_s[:, rows, :]
        u_s[:, rows, :] = h
        if t % 8 == 7:
            yield
    h_s[...] = h

    for b in range(B):
        hb = jnp.concatenate([u_s[j, pl.ds(b, T, stride=B), :] for j in range(n_slab)], axis=-1)
        o_ref[b] = _rms(hb * _silu(ga_ref[b].astype(F32)), g_ref[...]).astype(o_ref.dtype)
        yield


def _attn_stages(q_ref, k_ref, v_ref, o_ref, bias_s, x4_s, acc_s, m_s, l_s, *, S):
    qi = lax.broadcasted_iota(jnp.int32, (2 * BLK, 2 * BLK), 0) % BLK
    ki = lax.broadcasted_iota(jnp.int32, (2 * BLK, 2 * BLK), 1)
    dist = qi + BLK - ki
    bias_s[...] = jnp.where((dist >= 0) & (dist <= BLK), 0.0, NEG_INF).astype(bias_s.dtype)
    lane = lax.broadcasted_iota(jnp.int32, (1, LANES), 1)
    head_a = lane < HEAD_DIM

    Sq = S // 4

    def reorder(a, ref, r):
        x4_s[a, r * Sq:(r + 1) * Sq, :] = ref[0, pl.ds(r, Sq, stride=4), :]

    reorders = [functools.partial(reorder, a, ref, r)
                for a, ref in enumerate((q_ref, k_ref, v_ref)) for r in range(4)]

    natural = tuple((lambda rows, ref=ref: ref[0, rows, :]) for ref in (q_ref, k_ref, v_ref))
    mod4 = tuple((lambda rows, a=a: x4_s[a, rows, :]) for a in range(3))

    def tile(p, src, q_rows, k_rows, nk):
        qt = src[0](q_rows)
        kt = src[1](k_rows).astype(BF16)
        vt = src[2](k_rows).astype(BF16)
        q2 = jnp.concatenate([jnp.where(head_a, qt, 0.0), jnp.where(head_a, 0.0, qt)], axis=0)
        s = lax.dot_general(q2.astype(BF16), kt, (((1,), (1,)), ((), ())),
                            preferred_element_type=F32)
        yield
        sb = s.astype(BF16) + bias_s[:, 2 * BLK - nk:2 * BLK]
        m = jnp.max(sb, axis=-1, keepdims=True)
        e = jnp.exp2(sb - m)
        m = m.astype(F32)
        yield
        v_aug = jnp.concatenate([vt, jnp.ones((nk, LANES), BF16)], axis=1)
        r = jnp.dot(e, v_aug, preferred_element_type=F32)
        acc_s[p, q_rows, :] = jnp.where(head_a, r[0:BLK, 0:LANES], r[BLK:, 0:LANES])
        l_s[p, q_rows, :] = jnp.where(head_a, r[0:BLK, LANES:], r[BLK:, LANES:])
        m_s[p, q_rows, :] = jnp.where(head_a, m[0:BLK], m[BLK:])

    def band_tile(p, src, q0, first):
        q_rows = pl.ds(q0, BLK)
        if first:
            return tile(p, src, q_rows, q_rows, BLK)
        return tile(p, src, q_rows, pl.ds(q0 - BLK, 2 * BLK), 2 * BLK)

    n_blk = Sq // BLK
    tiles = [band_tile(0, natural, n * BLK, n == 0) for n in range(S // BLK)]
    tiles += [band_tile(1, mod4, r * Sq + n * BLK, n == 0) for r in range(4) for n in range(n_blk)]
    for r in range(4):
        for e in range(4):
            rows = pl.ds(r * Sq + e, BLK, stride=4)
            tiles.append(tile(2, mod4, rows, rows, BLK))

    rows_per = 256

    def combine(c):
        rows = pl.ds(c * rows_per, rows_per)
        nat_rows = pl.ds(c // 2 + (c % 2) * (4 * rows_per), rows_per, stride=4)
        m = [m_s[0, nat_rows, :], m_s[1, rows, :], m_s[2, rows, :]]
        mx = jnp.maximum(jnp.maximum(m[0], m[1]), m[2])
        w = [jnp.exp2(mp - mx) for mp in m]
        num = w[0] * acc_s[0, nat_rows, :] + w[1] * acc_s[1, rows, :] + w[2] * acc_s[2, rows, :]
        den = w[0] * l_s[0, nat_rows, :] + w[1] * l_s[1, rows, :] + w[2] * l_s[2, rows, :]
        o_ref[0, nat_rows, :] = (num / den).astype(o_ref.dtype)

    n_stage = 3
    chunks_per_res = Sq // rows_per
    first_d16 = len(tiles) - 16
    ready = {}
    for c in range(S // rows_per):
        last_tile = first_d16 + 4 * (c // chunks_per_res) + 3
        ready.setdefault(last_tile + (n_stage - 1) * SKEW + 1 + c % chunks_per_res, []).append(c)

    n_step = len(tiles) + (n_stage - 1) * SKEW
    assert len(reorders) <= S // BLK - 2
    for k in range(max(n_step, max(ready) + 1)):
        if k < len(reorders):
            reorders[k]()
        for stage in range(n_stage):
            if 0 <= k - stage * SKEW < len(tiles):
                next(tiles[k - stage * SKEW], None)
        for c in ready.get(k, []):
            combine(c)
        yield


def _mixers_kernel(q_ref, k_ref, v_ref, xa_ref, ga_ref, cw_ref, cb_ref, wg_ref, ba_ref,
                   bx_ref, lam_ref, g_ref, att_ref, ya_ref,
                   bias_s, x4_s, acc_s, m_s, l_s, xs, a_s, u_s, h_s, *, S, T):
    @pl.when(pl.program_id(0) == 0)
    def _():
        xs[:, 0:(CONV_WIDTH - 1) * SUBLANES, :] = jnp.zeros(
            (xs.shape[0], (CONV_WIDTH - 1) * SUBLANES, LANES), F32)
        h_s[...] = jnp.zeros(h_s.shape, F32)

    attn = _attn_stages(q_ref, k_ref, v_ref, att_ref, bias_s, x4_s, acc_s, m_s, l_s, S=S)
    rec = _rec_stages(xa_ref, ga_ref, cw_ref, cb_ref, wg_ref, ba_ref, bx_ref, lam_ref,
                      g_ref, ya_ref, xs, a_s, u_s, h_s, T=T)
    step = 0
    for _ in attn:
        if step % 2 == 1:
            next(rec, None)
        step += 1
    for _ in rec:
        pass


def _mixers_call(q, k, v, xa, ga, cw, cb, wg, ba, bx, lam, g):
    B, S, C = q.shape
    assert B == SUBLANES and xa.shape == (B, S, D_REC)
    n_pair = C // LANES
    n_step = B * n_pair
    T = S // n_step
    head = pl.BlockSpec((1, S, LANES), lambda s: (s // n_pair, 0, s % n_pair))
    seq = pl.BlockSpec((B, T, D_REC), lambda s: (0, s, 0))
    vec = pl.BlockSpec((1, D_REC), lambda s: (0, 0))
    n_slab = D_REC // LANES
    gates = pl.BlockSpec((n_slab, LANES, 2 * LANES), lambda s: (0, 0, 0))
    return pl.pallas_call(
        functools.partial(_mixers_kernel, S=S, T=T),
        grid=(n_step,),
        in_specs=[head, head, head, seq, seq,
                  pl.BlockSpec((CONV_WIDTH, D_REC), lambda s: (0, 0)), vec, gates,
                  vec, vec, vec, vec],
        out_specs=[head, seq],
        out_shape=[jax.ShapeDtypeStruct((B, S, C), F32), jax.ShapeDtypeStruct((B, S, D_REC), BF16)],
        scratch_shapes=[pltpu.VMEM((2 * BLK, 2 * BLK), BF16),
                        pltpu.VMEM((3, S, LANES), F32),
                        pltpu.VMEM((3, S, LANES), F32),
                        pltpu.VMEM((3, S, LANES), F32),
                        pltpu.VMEM((3, S, LANES), F32),
                        pltpu.VMEM((n_slab, (T + CONV_WIDTH - 1) * B, LANES), F32),
                        pltpu.VMEM((n_slab, T * B, LANES), F32),
                        pltpu.VMEM((n_slab, T * B, LANES), F32),
                        pltpu.VMEM((n_slab, B, LANES), F32)],
        compiler_params=pltpu.CompilerParams(
            dimension_semantics=("arbitrary",), vmem_limit_bytes=VMEM_LIMIT),
        name="mixers",
    )(q, k, v, xa, ga, cw, cb.reshape(1, D_REC), wg, ba.reshape(1, D_REC),
      bx.reshape(1, D_REC), lam.reshape(1, D_REC), g.reshape(1, D_REC))


def _outproj_rows(rows, x_ref, ya_ref, att_ref, gb_ref, mod_ref, natt_ref, npost_ref, w1_ref,
                  w2_ref, o_ref):
    ya = ya_ref[0, rows, :]
    yb = _rms(att_ref[0, rows, :] * _silu(gb_ref[0, rows, :].astype(F32)), natt_ref[...]).astype(BF16)
    yield
    mix = (jnp.dot(ya, w1_ref[...], preferred_element_type=F32)
           + jnp.dot(yb, w2_ref[...], preferred_element_type=F32))
    yield
    gate = mod_ref[0, :, 2 * D_MODEL:3 * D_MODEL]
    o_ref[0, rows, :] = x_ref[0, rows, :] + gate * _rms(mix, npost_ref[...])


def _outproj_kernel(x_ref, *refs):
    ts = x_ref.shape[1]
    _skewed([_outproj_rows(pl.ds(r0, SUB_ROWS), x_ref, *refs) for r0 in range(0, ts, SUB_ROWS)], 3)


def _outproj_call(x, ya, att, gb, mod, natt, npost, w1, w2, ts):
    B, S, D = x.shape
    seq = lambda b, i: (b, i, 0)
    const = lambda b, i: (0, 0)
    half = pl.BlockSpec((1, ts, D_REC), seq)
    return pl.pallas_call(
        _outproj_kernel,
        grid=(B, S // ts),
        in_specs=[pl.BlockSpec((1, ts, D), seq), half, half, half,
                  pl.BlockSpec((1, 1, 3 * D), lambda b, i: (b, 0, 0)),
                  pl.BlockSpec((1, D_ATT), const),
                  pl.BlockSpec((1, D), const),
                  pl.BlockSpec((D_REC, D), const),
                  pl.BlockSpec((D_ATT, D), const)],
        out_specs=pl.BlockSpec((1, ts, D), seq),
        out_shape=jax.ShapeDtypeStruct((B, S, D), F32),
        compiler_params=pltpu.CompilerParams(
            dimension_semantics=("parallel", "parallel"), vmem_limit_bytes=VMEM_LIMIT),
        name="outproj",
    )(x, ya, att, gb, mod.reshape(B, 1, 3 * D), natt.reshape(1, D_ATT), npost.reshape(1, D), w1, w2)


def _gate_weights(w_a, w_x):
    nb, n, _ = w_a.shape
    per = LANES // n
    eye = jnp.eye(per, dtype=w_a.dtype)

    def lane_blocks(w):
        w = w.reshape(nb // per, per, n, n)
        return jnp.einsum('jhik,hg->jhigk', w, eye).reshape(nb // per, LANES, LANES)

    return (0.5 * jnp.concatenate([lane_blocks(w_a), lane_blocks(w_x)], axis=-1)).astype(BF16)


def kernel(x, c, positions, w_ada, b_ada, norm_pre, norm_post, w_in, conv_w, conv_b, w_rg_a, b_rg_a,
           w_rg_x, b_rg_x, lru_lambda, norm_rec, norm_att, w_out):
    depth = w_in.shape[0]
    inv_freq = ROPE_THETA ** (-jnp.arange(HALF, dtype=F32) / HALF)
    invf = jnp.tile(inv_freq, LANES // HALF).reshape(1, LANES)
    for l in range(depth):
        mod = _mod_call(c, w_ada[l], b_ada[l])
        xa, ga, q, k, v, gb = _inproj_call(x, mod, norm_pre[l], positions, invf,
                                           w_in[l].astype(BF16), ts=512)
        att, ya = _mixers_call(q, k, v, xa, ga, conv_w[l], conv_b[l],
                               _gate_weights(w_rg_a[l], w_rg_x[l]),
                               b_rg_a[l], b_rg_x[l], lru_lambda[l], norm_rec[l])
        x = _outproj_call(x, ya, att, gb, mod, norm_att[l], norm_post[l],
                          w_out[l, :D_REC].astype(BF16), w_out[l, D_REC:].astype(BF16), ts=512)
    return x
```

```python
import functools

import jax
import jax.numpy as jnp
from jax import lax
from jax.experimental import pallas as pl
from jax.experimental.pallas import tpu as pltpu

F32 = jnp.float32
BF16 = jnp.bfloat16

D_MODEL = 1024
D_REC = 512
D_ATT = 512
N_LRU_BLOCKS = 8
LRU_C = 8.0
CONV_WIDTH = 4
HEAD_DIM = 64
HALF = HEAD_DIM // 2
ROPE_THETA = 10000.0
NORM_EPS = 1e-6
NEG_INF = -1e30
D_IN_PROJ = 2 * D_REC + 4 * D_ATT

LANES = 128
SUBLANES = 8
BLK = 128
DILATIONS = (1, 4, 16)
SKEW = 1

VMEM_LIMIT = 56 * 1024 * 1024
Q_SCALE = HEAD_DIM ** -0.5 * 1.4426950408889634


def _sigmoid(x):
    return 0.5 * jnp.tanh(0.5 * x) + 0.5


def _silu(x):
    h = 0.5 * x
    return h + h * jnp.tanh(h)


def _rms(x, g):
    return x * lax.rsqrt(jnp.mean(x * x, axis=-1, keepdims=True) + NORM_EPS) * g


def _mod_kernel(c_ref, w_ref, b_ref, o_ref):
    o_ref[...] = jnp.dot(_silu(c_ref[...]).astype(BF16), w_ref[...].astype(BF16),
                         preferred_element_type=F32) + b_ref[...]


def _mod_call(c, w, b):
    B, D = c.shape
    N = w.shape[1]
    tn = 1024
    return pl.pallas_call(
        _mod_kernel,
        grid=(N // tn,),
        in_specs=[pl.BlockSpec((B, D), lambda j: (0, 0)),
                  pl.BlockSpec((D, tn), lambda j: (0, j)),
                  pl.BlockSpec((1, tn), lambda j: (0, j))],
        out_specs=pl.BlockSpec((B, tn), lambda j: (0, j)),
        out_shape=jax.ShapeDtypeStruct((B, N), F32),
        compiler_params=pltpu.CompilerParams(vmem_limit_bytes=VMEM_LIMIT),
        name="mod",
    )(c, w, b.reshape(1, N))


def _skewed(stages, n_stage):
    for k in range(len(stages) + n_stage - 1):
        for stage in range(n_stage):
            if 0 <= k - stage < len(stages):
                next(stages[k - stage], None)


SUB_ROWS = 256


def _inproj_rows(r0, x_ref, mod_ref, g_ref, pos_ref, invf_ref, w_ref,
                 xa_ref, ga_ref, q_ref, k_ref, v_ref, gb_ref):
    rows = pl.ds(r0, SUB_ROWS)
    shift = mod_ref[0, :, 0:D_MODEL]
    scale = mod_ref[0, :, D_MODEL:2 * D_MODEL]
    h = (_rms(x_ref[0, rows, :], g_ref[...]) * (1.0 + scale) + shift).astype(BF16)
    yield
    proj = jnp.dot(h, w_ref[...], preferred_element_type=F32)
    yield
    xa_ref[0, rows, :] = proj[:, 0:D_REC]
    ga_ref[0, rows, :] = proj[:, D_REC:2 * D_REC]
    o = 2 * D_REC
    v_ref[0, rows, :] = proj[:, o + 2 * D_ATT:o + 3 * D_ATT]
    gb_ref[0, rows, :] = proj[:, o + 3 * D_ATT:o + 4 * D_ATT].astype(gb_ref.dtype)

    pos = pos_ref[pl.ds(pl.program_id(0), 1), rows].astype(F32)
    pos = jnp.concatenate([jnp.broadcast_to(pos[:, c:c + LANES], (LANES, LANES)).T
                           for c in range(0, SUB_ROWS, LANES)], axis=0)
    ang = pos * invf_ref[...]
    cos = jnp.cos(ang)
    sin = jnp.sin(ang)
    lane = lax.broadcasted_iota(jnp.int32, (1, LANES), 1)
    first = (lane % HEAD_DIM) < HALF
    sin_signed = jnp.where(first, -sin, sin)

    def rope(t):
        partner = jnp.where(first, pltpu.roll(t, LANES - HALF, 1), pltpu.roll(t, HALF, 1))
        return t * cos + partner * sin_signed

    for j in range(D_ATT // LANES):
        sl = slice(j * LANES, (j + 1) * LANES)
        q_ref[0, rows, sl] = rope(proj[:, o + j * LANES:o + (j + 1) * LANES]) * Q_SCALE
        k_ref[0, rows, sl] = rope(proj[:, o + D_ATT + j * LANES:o + D_ATT + (j + 1) * LANES])


def _inproj_kernel(x_ref, *refs):
    ts = x_ref.shape[1]
    _skewed([_inproj_rows(r0, x_ref, *refs) for r0 in range(0, ts, SUB_ROWS)], 3)


def _inproj_call(x, mod, g, pos, invf, w, ts):
    B, S, D = x.shape
    seq = lambda b, i: (b, i, 0)
    const = lambda b, i: (0, 0)
    half = pl.BlockSpec((1, ts, D_REC), seq)
    half_shape = jax.ShapeDtypeStruct((B, S, D_REC), F32)
    return pl.pallas_call(
        _inproj_kernel,
        grid=(B, S // ts),
        in_specs=[pl.BlockSpec((1, ts, D), seq),
                  pl.BlockSpec((1, 1, 3 * D), lambda b, i: (b, 0, 0)),
                  pl.BlockSpec((1, D), const),
                  pl.BlockSpec((B, ts), lambda b, i: (0, i)),
                  pl.BlockSpec((1, LANES), const),
                  pl.BlockSpec((D, D_IN_PROJ), const)],
        out_specs=[half] * 6,
        out_shape=[half_shape] * 5 + [jax.ShapeDtypeStruct((B, S, D_ATT), BF16)],
        compiler_params=pltpu.CompilerParams(
            dimension_semantics=("parallel", "parallel"), vmem_limit_bytes=VMEM_LIMIT),
        name="inproj",
    )(x, mod.reshape(B, 1, 3 * D), g.reshape(1, D), pos, invf, w)


def _rec_stages(xa_ref, ga_ref, cw_ref, cb_ref, wg_ref, ba_ref, bx_ref, lam_ref, g_ref,
                o_ref, xs, a_s, u_s, h_s, *, T):
    B = SUBLANES
    tail = (CONV_WIDTH - 1) * B
    n_slab = D_REC // LANES
    slab = lambda j: slice(j * LANES, (j + 1) * LANES)

    for b in range(B):
        for j in range(n_slab):
            xs[j, pl.ds(tail + b, T, stride=B), :] = xa_ref[b, :, slab(j)]
        yield

    z = -lam_ref[...]
    half_rate = (-0.5 * LRU_C) * (jnp.maximum(z, 0.0) + jnp.log1p(jnp.exp(-jnp.abs(z))))
    half_ba = 0.5 * ba_ref[...]
    half_bx = 0.5 * bx_ref[...]
    chunk = 16 * B
    for r0 in range(0, T * B, chunk):
        for j in range(n_slab):
            xc = sum(cw_ref[k:k + 1, slab(j)] * xs[j, k * B + r0:k * B + r0 + chunk, :]
                     for k in range(CONV_WIDTH)) + cb_ref[:, slab(j)]
            pre = jnp.dot(xc.astype(BF16), wg_ref[j], preferred_element_type=F32)
            ta = jnp.tanh(pre[:, 0:LANES] + half_ba[:, slab(j)])
            ig = 0.5 * jnp.tanh(pre[:, LANES:] + half_bx[:, slab(j)]) + 0.5
            log_a = ta * half_rate[:, slab(j)] + half_rate[:, slab(j)]
            a = jnp.exp(log_a)
            a_s[j, r0:r0 + chunk, :] = a
            u_s[j, r0:r0 + chunk, :] = jnp.sqrt(-jnp.tanh(log_a) * (1.0 + a * a)) * (ig * xc)
        yield
    xs[:, 0:tail, :] = xs[:, T * B:T * B + tail, :]

    h = h_s[...]
    for t in range(T):
        rows = pl.ds(t * B, B)
        h = a_s[:, rows, :] * h + u_s[:, rows, :]
        u_s[:, rows, :] = h
        if t % 8 == 7:
            yield
    h_s[...] = h

    for b in range(B):
        hb = jnp.concatenate([u_s[j, pl.ds(b, T, stride=B), :] for j in range(n_slab)], axis=-1)
        o_ref[b] = _rms(hb * _silu(ga_ref[b].astype(F32)), g_ref[...]).astype(o_ref.dtype)
        yield


def _attn_stages(q_ref, k_ref, v_ref, o_ref, bias_s, x4_s, acc_s, m_s, l_s, *, S):
    qi = lax.broadcasted_iota(jnp.int32, (2 * BLK, 2 * BLK), 0) % BLK
    ki = lax.broadcasted_iota(jnp.int32, (2 * BLK, 2 * BLK), 1)
    dist = qi + BLK - ki
    bias_s[...] = jnp.where((dist >= 0) & (dist <= BLK), 0.0, NEG_INF).astype(bias_s.dtype)
    lane = lax.broadcasted_iota(jnp.int32, (1, LANES), 1)
    head_a = lane < HEAD_DIM

    Sq = S // 4

    def reorder(a, ref, r):
        x4_s[a, r * Sq:(r + 1) * Sq, :] = ref[0, pl.ds(r, Sq, stride=4), :]

    reorders = [functools.partial(reorder, a, ref, r)
                for a, ref in enumerate((q_ref, k_ref, v_ref)) for r in range(4)]

    natural = tuple((lambda rows, ref=ref: ref[0, rows, :]) for ref in (q_ref, k_ref, v_ref))
    mod4 = tuple((lambda rows, a=a: x4_s[a, rows, :]) for a in range(3))

    def tile(p, src, q_rows, k_rows, nk):
        qt = src[0](q_rows)
        kt = src[1](k_rows).astype(BF16)
        vt = src[2](k_rows).astype(BF16)
        q2 = jnp.concatenate([jnp.where(head_a, qt, 0.0), jnp.where(head_a, 0.0, qt)], axis=0)
        s = lax.dot_general(q2.astype(BF16), kt, (((1,), (1,)), ((), ())),
                            preferred_element_type=F32)
        yield
        sb = s.astype(BF16) + bias_s[:, 2 * BLK - nk:2 * BLK]
        m = jnp.max(sb, axis=-1, keepdims=True)
        e = jnp.exp2(sb - m)
        m = m.astype(F32)
        yield
        v_aug = jnp.concatenate([vt, jnp.ones((nk, LANES), BF16)], axis=1)
        r = jnp.dot(e, v_aug, preferred_element_type=F32)
        acc_s[p, q_rows, :] = jnp.where(head_a, r[0:BLK, 0:LANES], r[BLK:, 0:LANES])
        l_s[p, q_rows, :] = jnp.where(head_a, r[0:BLK, LANES:], r[BLK:, LANES:])
        m_s[p, q_rows, :] = jnp.where(head_a, m[0:BLK], m[BLK:])

    def band_tile(p, src, q0, first):
        q_rows = pl.ds(q0, BLK)
        if first:
            return tile(p, src, q_rows, q_rows, BLK)
        return tile(p, src, q_rows, pl.ds(q0 - BLK, 2 * BLK), 2 * BLK)

    n_blk = Sq // BLK
    tiles = [band_tile(0, natural, n * BLK, n == 0) for n in range(S // BLK)]
    tiles += [band_tile(1, mod4, r * Sq + n * BLK, n == 0) for r in range(4) for n in range(n_blk)]
    for r in range(4):
        for e in range(4):
            rows = pl.ds(r * Sq + e, BLK, stride=4)
            tiles.append(tile(2, mod4, rows, rows, BLK))

    rows_per = 256

    def combine(c):
        rows = pl.ds(c * rows_per, rows_per)
        nat_rows = pl.ds(c // 2 + (c % 2) * (4 * rows_per), rows_per, stride=4)
        m = [m_s[0, nat_rows, :], m_s[1, rows, :], m_s[2, rows, :]]
        mx = jnp.maximum(jnp.maximum(m[0], m[1]), m[2])
        w = [jnp.exp2(mp - mx) for mp in m]
        num = w[0] * acc_s[0, nat_rows, :] + w[1] * acc_s[1, rows, :] + w[2] * acc_s[2, rows, :]
        den = w[0] * l_s[0, nat_rows, :] + w[1] * l_s[1, rows, :] + w[2] * l_s[2, rows, :]
        o_ref[0, nat_rows, :] = (num / den).astype(o_ref.dtype)

    n_stage = 3
    chunks_per_res = Sq // rows_per
    first_d16 = len(tiles) - 16
    ready = {}
    for c in range(S // rows_per):
        last_tile = first_d16 + 4 * (c // chunks_per_res) + 3
        ready.setdefault(last_tile + (n_stage - 1) * SKEW + 1 + c % chunks_per_res, []).append(c)

    n_step = len(tiles) + (n_stage - 1) * SKEW
    assert len(reorders) <= S // BLK - 2
    for k in range(max(n_step, max(ready) + 1)):
        if k < len(reorders):
            reorders[k]()
        for stage in range(n_stage):
            if 0 <= k - stage * SKEW < len(tiles):
                next(tiles[k - stage * SKEW], None)
        for c in ready.get(k, []):
            combine(c)
        yield


def _mixers_kernel(q_ref, k_ref, v_ref, xa_ref, ga_ref, cw_ref, cb_ref, wg_ref, ba_ref,
                   bx_ref, lam_ref, g_ref, att_ref, ya_ref,
                   bias_s, x4_s, acc_s, m_s, l_s, xs, a_s, u_s, h_s, *, S, T):
    @pl.when(pl.program_id(0) == 0)
    def _():
        xs[:, 0:(CONV_WIDTH - 1) * SUBLANES, :] = jnp.zeros(
            (xs.shape[0], (CONV_WIDTH - 1) * SUBLANES, LANES), F32)
        h_s[...] = jnp.zeros(h_s.shape, F32)

    attn = _attn_stages(q_ref, k_ref, v_ref, att_ref, bias_s, x4_s, acc_s, m_s, l_s, S=S)
    rec = _rec_stages(xa_ref, ga_ref, cw_ref, cb_ref, wg_ref, ba_ref, bx_ref, lam_ref,
                      g_ref, ya_ref, xs, a_s, u_s, h_s, T=T)
    step = 0
    for _ in attn:
        if step % 2 == 1:
            next(rec, None)
        step += 1
    for _ in rec:
        pass


def _mixers_call(q, k, v, xa, ga, cw, cb, wg, ba, bx, lam, g):
    B, S, C = q.shape
    assert B == SUBLANES and xa.shape == (B, S, D_REC)
    n_pair = C // LANES
    n_step = B * n_pair
    T = S // n_step
    head = pl.BlockSpec((1, S, LANES), lambda s: (s // n_pair, 0, s % n_pair))
    seq = pl.BlockSpec((B, T, D_REC), lambda s: (0, s, 0))
    vec = pl.BlockSpec((1, D_REC), lambda s: (0, 0))
    n_slab = D_REC // LANES
    gates = pl.BlockSpec((n_slab, LANES, 2 * LANES), lambda s: (0, 0, 0))
    return pl.pallas_call(
        functools.partial(_mixers_kernel, S=S, T=T),
        grid=(n_step,),
        in_specs=[head, head, head, seq, seq,
                  pl.BlockSpec((CONV_WIDTH, D_REC), lambda s: (0, 0)), vec, gates,
                  vec, vec, vec, vec],
        out_specs=[head, seq],
        out_shape=[jax.ShapeDtypeStruct((B, S, C), F32), jax.ShapeDtypeStruct((B, S, D_REC), BF16)],
        scratch_shapes=[pltpu.VMEM((2 * BLK, 2 * BLK), BF16),
                        pltpu.VMEM((3, S, LANES), F32),
                        pltpu.VMEM((3, S, LANES), F32),
                        pltpu.VMEM((3, S, LANES), F32),
                        pltpu.VMEM((3, S, LANES), F32),
                        pltpu.VMEM((n_slab, (T + CONV_WIDTH - 1) * B, LANES), F32),
                        pltpu.VMEM((n_slab, T * B, LANES), F32),
                        pltpu.VMEM((n_slab, T * B, LANES), F32),
                        pltpu.VMEM((n_slab, B, LANES), F32)],
        compiler_params=pltpu.CompilerParams(
            dimension_semantics=("arbitrary",), vmem_limit_bytes=VMEM_LIMIT),
        name="mixers",
    )(q, k, v, xa, ga, cw, cb.reshape(1, D_REC), wg, ba.reshape(1, D_REC),
      bx.reshape(1, D_REC), lam.reshape(1, D_REC), g.reshape(1, D_REC))


def _outproj_rows(rows, x_ref, ya_ref, att_ref, gb_ref, mod_ref, natt_ref, npost_ref, w1_ref,
                  w2_ref, o_ref):
    ya = ya_ref[0, rows, :]
    yb = _rms(att_ref[0, rows, :] * _silu(gb_ref[0, rows, :].astype(F32)), natt_ref[...]).astype(BF16)
    yield
    mix = (jnp.dot(ya, w1_ref[...], preferred_element_type=F32)
           + jnp.dot(yb, w2_ref[...], preferred_element_type=F32))
    yield
    gate = mod_ref[0, :, 2 * D_MODEL:3 * D_MODEL]
    o_ref[0, rows, :] = x_ref[0, rows, :] + gate * _rms(mix, npost_ref[...])


def _outproj_kernel(x_ref, *refs):
    ts = x_ref.shape[1]
    _skewed([_outproj_rows(pl.ds(r0, SUB_ROWS), x_ref, *refs) for r0 in range(0, ts, SUB_ROWS)], 3)


def _outproj_call(x, ya, att, gb, mod, natt, npost, w1, w2, ts):
    B, S, D = x.shape
    seq = lambda b, i: (b, i, 0)
    const = lambda b, i: (0, 0)
    half = pl.BlockSpec((1, ts, D_REC), seq)
    return pl.pallas_call(
        _outproj_kernel,
        grid=(B, S // ts),
        in_specs=[pl.BlockSpec((1, ts, D), seq), half, half, half,
                  pl.BlockSpec((1, 1, 3 * D), lambda b, i: (b, 0, 0)),
                  pl.BlockSpec((1, D_ATT), const),
                  pl.BlockSpec((1, D), const),
                  pl.BlockSpec((D_REC, D), const),
                  pl.BlockSpec((D_ATT, D), const)],
        out_specs=pl.BlockSpec((1, ts, D), seq),
        out_shape=jax.ShapeDtypeStruct((B, S, D), F32),
        compiler_params=pltpu.CompilerParams(
            dimension_semantics=("parallel", "parallel"), vmem_limit_bytes=VMEM_LIMIT),
        name="outproj",
    )(x, ya, att, gb, mod.reshape(B, 1, 3 * D), natt.reshape(1, D_ATT), npost.reshape(1, D), w1, w2)


def _gate_weights(w_a, w_x):
    nb, n, _ = w_a.shape
    per = LANES // n
    eye = jnp.eye(per, dtype=w_a.dtype)

    def lane_blocks(w):
        w = w.reshape(nb // per, per, n, n)
        return jnp.einsum('jhik,hg->jhigk', w, eye).reshape(nb // per, LANES, LANES)

    return (0.5 * jnp.concatenate([lane_blocks(w_a), lane_blocks(w_x)], axis=-1)).astype(BF16)


def kernel(x, c, positions, w_ada, b_ada, norm_pre, norm_post, w_in, conv_w, conv_b, w_rg_a, b_rg_a,
           w_rg_x, b_rg_x, lru_lambda, norm_rec, norm_att, w_out):
    depth = w_in.shape[0]
    inv_freq = ROPE_THETA ** (-jnp.arange(HALF, dtype=F32) / HALF)
    invf = jnp.tile(inv_freq, LANES // HALF).reshape(1, LANES)
    for l in range(depth):
        mod = _mod_call(c, w_ada[l], b_ada[l])
        xa, ga, q, k, v, gb = _inproj_call(x, mod, norm_pre[l], positions, invf,
                                           w_in[l].astype(BF16), ts=512)
        att, ya = _mixers_call(q, k, v, xa, ga, conv_w[l], conv_b[l],
                               _gate_weights(w_rg_a[l], w_rg_x[l]),
                               b_rg_a[l], b_rg_x[l], lru_lambda[l], norm_rec[l])
        x = _outproj_call(x, ya, att, gb, mod, norm_att[l], norm_post[l],
                          w_out[l, :D_REC].astype(BF16), w_out[l, D_REC:].astype(BF16), ts=1024)
    return x
```

```python
import functools

import jax
import jax.numpy as jnp
from jax import lax
from jax.experimental import pallas as pl
from jax.experimental.pallas import tpu as pltpu

F32 = jnp.float32
BF16 = jnp.bfloat16

D_MODEL = 1024
D_REC = 512
D_ATT = 512
N_LRU_BLOCKS = 8
LRU_C = 8.0
CONV_WIDTH = 4
HEAD_DIM = 64
HALF = HEAD_DIM // 2
ROPE_THETA = 10000.0
NORM_EPS = 1e-6
NEG_INF = -1e30
D_IN_PROJ = 2 * D_REC + 4 * D_ATT

LANES = 128
SUBLANES = 8
BLK = 128
DILATIONS = (1, 4, 16)
SKEW = 1

VMEM_LIMIT = 56 * 1024 * 1024
Q_SCALE = HEAD_DIM ** -0.5 * 1.4426950408889634


def _sigmoid(x):
    return 0.5 * jnp.tanh(0.5 * x) + 0.5


def _silu(x):
    h = 0.5 * x
    return h + h * jnp.tanh(h)


def _rms(x, g):
    return x * lax.rsqrt(jnp.mean(x * x, axis=-1, keepdims=True) + NORM_EPS) * g


def _mod_kernel(c_ref, w_ref, b_ref, o_ref):
    o_ref[...] = jnp.dot(_silu(c_ref[...]).astype(BF16), w_ref[...].astype(BF16),
                         preferred_element_type=F32) + b_ref[...]


def _mod_call(c, w, b):
    B, D = c.shape
    N = w.shape[1]
    tn = 1024
    return pl.pallas_call(
        _mod_kernel,
        grid=(N // tn,),
        in_specs=[pl.BlockSpec((B, D), lambda j: (0, 0)),
                  pl.BlockSpec((D, tn), lambda j: (0, j)),
                  pl.BlockSpec((1, tn), lambda j: (0, j))],
        out_specs=pl.BlockSpec((B, tn), lambda j: (0, j)),
        out_shape=jax.ShapeDtypeStruct((B, N), F32),
        compiler_params=pltpu.CompilerParams(vmem_limit_bytes=VMEM_LIMIT),
        name="mod",
    )(c, w, b.reshape(1, N))


def _skewed(stages, n_stage):
    for k in range(len(stages) + n_stage - 1):
        for stage in range(n_stage):
            if 0 <= k - stage < len(stages):
                next(stages[k - stage], None)


SUB_ROWS = 256


def _inproj_rows(r0, x_ref, mod_ref, g_ref, pos_ref, invf_ref, w_ref,
                 xa_ref, ga_ref, q_ref, k_ref, v_ref, gb_ref):
    rows = pl.ds(r0, SUB_ROWS)
    shift = mod_ref[0, :, 0:D_MODEL]
    scale = mod_ref[0, :, D_MODEL:2 * D_MODEL]
    h = (_rms(x_ref[0, rows, :], g_ref[...]) * (1.0 + scale) + shift).astype(BF16)
    yield
    proj = jnp.dot(h, w_ref[...], preferred_element_type=F32)
    yield
    xa_ref[0, rows, :] = proj[:, 0:D_REC]
    ga_ref[0, rows, :] = proj[:, D_REC:2 * D_REC]
    o = 2 * D_REC
    v_ref[0, rows, :] = proj[:, o + 2 * D_ATT:o + 3 * D_ATT]
    gb_ref[0, rows, :] = proj[:, o + 3 * D_ATT:o + 4 * D_ATT].astype(gb_ref.dtype)

    pos = pos_ref[pl.ds(pl.program_id(0), 1), rows].astype(F32)
    pos = jnp.concatenate([jnp.broadcast_to(pos[:, c:c + LANES], (LANES, LANES)).T
                           for c in range(0, SUB_ROWS, LANES)], axis=0)
    ang = pos * invf_ref[...]
    cos = jnp.cos(ang)
    sin = jnp.sin(ang)
    lane = lax.broadcasted_iota(jnp.int32, (1, LANES), 1)
    first = (lane % HEAD_DIM) < HALF
    sin_signed = jnp.where(first, -sin, sin)

    def rope(t):
        partner = jnp.where(first, pltpu.roll(t, LANES - HALF, 1), pltpu.roll(t, HALF, 1))
        return t * cos + partner * sin_signed

    for j in range(D_ATT // LANES):
        sl = slice(j * LANES, (j + 1) * LANES)
        q_ref[0, rows, sl] = rope(proj[:, o + j * LANES:o + (j + 1) * LANES]) * Q_SCALE
        k_ref[0, rows, sl] = rope(proj[:, o + D_ATT + j * LANES:o + D_ATT + (j + 1) * LANES])


def _inproj_kernel(x_ref, *refs):
    ts = x_ref.shape[1]
    _skewed([_inproj_rows(r0, x_ref, *refs) for r0 in range(0, ts, SUB_ROWS)], 3)


def _inproj_call(x, mod, g, pos, invf, w, ts):
    B, S, D = x.shape
    seq = lambda b, i: (b, i, 0)
    const = lambda b, i: (0, 0)
    half = pl.BlockSpec((1, ts, D_REC), seq)
    half_shape = jax.ShapeDtypeStruct((B, S, D_REC), F32)
    return pl.pallas_call(
        _inproj_kernel,
        grid=(B, S // ts),
        in_specs=[pl.BlockSpec((1, ts, D), seq),
                  pl.BlockSpec((1, 1, 3 * D), lambda b, i: (b, 0, 0)),
                  pl.BlockSpec((1, D), const),
                  pl.BlockSpec((B, ts), lambda b, i: (0, i)),
                  pl.BlockSpec((1, LANES), const),
                  pl.BlockSpec((D, D_IN_PROJ), const)],
        out_specs=[half] * 6,
        out_shape=[half_shape] * 5 + [jax.ShapeDtypeStruct((B, S, D_ATT), BF16)],
        compiler_params=pltpu.CompilerParams(
            dimension_semantics=("parallel", "parallel"), vmem_limit_bytes=VMEM_LIMIT),
        name="inproj",
    )(x, mod.reshape(B, 1, 3 * D), g.reshape(1, D), pos, invf, w)


def _rec_stages(xa_ref, ga_ref, cw_ref, cb_ref, wg_ref, ba_ref, bx_ref, lam_ref, g_ref,
                o_ref, xs, a_s, u_s, h_s, *, T):
    B = SUBLANES
    tail = (CONV_WIDTH - 1) * B
    n_slab = D_REC // LANES
    slab = lambda j: slice(j * LANES, (j + 1) * LANES)

    for b in range(B):
        for j in range(n_slab):
            xs[j, pl.ds(tail + b, T, stride=B), :] = xa_ref[b, :, slab(j)]
        yield

    z = -lam_ref[...]
    half_rate = (-0.5 * LRU_C) * (jnp.maximum(z, 0.0) + jnp.log1p(jnp.exp(-jnp.abs(z))))
    half_ba = 0.5 * ba_ref[...]
    half_bx = 0.5 * bx_ref[...]
    chunk = 16 * B
    for r0 in range(0, T * B, chunk):
        for j in range(n_slab):
            xc = sum(cw_ref[k:k + 1, slab(j)] * xs[j, k * B + r0:k * B + r0 + chunk, :]
                     for k in range(CONV_WIDTH)) + cb_ref[:, slab(j)]
            pre = jnp.dot(xc.astype(BF16), wg_ref[j], preferred_element_type=F32)
            ta = jnp.tanh(pre[:, 0:LANES] + half_ba[:, slab(j)])
            ig = 0.5 * jnp.tanh(pre[:, LANES:] + half_bx[:, slab(j)]) + 0.5
            log_a = ta * half_rate[:, slab(j)] + half_rate[:, slab(j)]
            a = jnp.exp(log_a)
            a_s[j, r0:r0 + chunk, :] = a
            u_s[j, r0:r0 + chunk, :] = jnp.sqrt(-jnp.tanh(log_a) * (1.0 + a * a)) * (ig * xc)
        yield
    xs[:, 0:tail, :] = xs[:, T * B:T * B + tail, :]

    h = h_s[...]
    for t in range(T):
        rows = pl.ds(t * B, B)
        h = a_s[:, rows, :] * h + u_s[:, rows, :]
        u_s[:, rows, :] = h
        if t % 8 == 7:
            yield
    h_s[...] = h

    for b in range(B):
        hb = jnp.concatenate([u_s[j, pl.ds(b, T, stride=B), :] for j in range(n_slab)], axis=-1)
        o_ref[b] = _rms(hb * _silu(ga_ref[b].astype(F32)), g_ref[...]).astype(o_ref.dtype)
        yield


def _attn_stages(q_ref, k_ref, v_ref, o_ref, bias_s, x4_s, acc_s, m_s, l_s, *, S):
    qi = lax.broadcasted_iota(jnp.int32, (2 * BLK, 2 * BLK), 0) % BLK
    ki = lax.broadcasted_iota(jnp.int32, (2 * BLK, 2 * BLK), 1)
    dist = qi + BLK - ki
    bias_s[...] = jnp.where((dist >= 0) & (dist <= BLK), 0.0, NEG_INF).astype(bias_s.dtype)
    lane = lax.broadcasted_iota(jnp.int32, (1, LANES), 1)
    head_a = lane < HEAD_DIM
    ind_a = jnp.where(head_a, 1.0, 0.0).astype(BF16)
    ind_b = jnp.where(head_a, 0.0, 1.0).astype(BF16)

    Sq = S // 4

    def reorder(a, ref, r):
        x4_s[a, r * Sq:(r + 1) * Sq, :] = ref[0, pl.ds(r, Sq, stride=4), :]

    reorders = [functools.partial(reorder, a, ref, r)
                for a, ref in enumerate((q_ref, k_ref, v_ref)) for r in range(4)]

    natural = tuple((lambda rows, ref=ref: ref[0, rows, :]) for ref in (q_ref, k_ref, v_ref))
    mod4 = tuple((lambda rows, a=a: x4_s[a, rows, :]) for a in range(3))

    def tile(p, src, q_rows, k_rows, nk):
        qt = src[0](q_rows)
        kt = src[1](k_rows).astype(BF16)
        vt = src[2](k_rows).astype(BF16)
        q2 = jnp.concatenate([jnp.where(head_a, qt, 0.0), jnp.where(head_a, 0.0, qt)], axis=0)
        s = lax.dot_general(q2.astype(BF16), kt, (((1,), (1,)), ((), ())),
                            preferred_element_type=F32)
        yield
        sb = s.astype(BF16) + bias_s[:, 2 * BLK - nk:2 * BLK]
        m = jnp.max(sb, axis=-1, keepdims=True)
        e = jnp.exp2(sb - m)
        m = m.astype(F32)
        m_s[p, q_rows, :] = jnp.where(head_a, m[0:BLK], m[BLK:])
        yield
        v_a = jnp.concatenate([vt * ind_a, jnp.broadcast_to(ind_a, (nk, LANES))], axis=1)
        v_b = jnp.concatenate([vt * ind_b, jnp.broadcast_to(ind_b, (nk, LANES))], axis=1)
        r = jnp.dot(jnp.concatenate([e[0:BLK], e[BLK:]], axis=1), jnp.concatenate([v_a, v_b], axis=0),
                    preferred_element_type=F32)
        acc_s[p, q_rows, :] = r[:, 0:LANES]
        l_s[p, q_rows, :] = r[:, LANES:]

    def band_tile(p, src, q0, first):
        q_rows = pl.ds(q0, BLK)
        if first:
            return tile(p, src, q_rows, q_rows, BLK)
        return tile(p, src, q_rows, pl.ds(q0 - BLK, 2 * BLK), 2 * BLK)

    n_blk = Sq // BLK
    tiles = [band_tile(0, natural, n * BLK, n == 0) for n in range(S // BLK)]
    tiles += [band_tile(1, mod4, r * Sq + n * BLK, n == 0) for r in range(4) for n in range(n_blk)]
    for r in range(4):
        for e in range(4):
            rows = pl.ds(r * Sq + e, BLK, stride=4)
            tiles.append(tile(2, mod4, rows, rows, BLK))

    rows_per = 256

    def combine(c):
        rows = pl.ds(c * rows_per, rows_per)
        nat_rows = pl.ds(c // 2 + (c % 2) * (4 * rows_per), rows_per, stride=4)
        m = [m_s[0, nat_rows, :], m_s[1, rows, :], m_s[2, rows, :]]
        mx = jnp.maximum(jnp.maximum(m[0], m[1]), m[2])
        w = [jnp.exp2(mp - mx) for mp in m]
        num = w[0] * acc_s[0, nat_rows, :] + w[1] * acc_s[1, rows, :] + w[2] * acc_s[2, rows, :]
        den = w[0] * l_s[0, nat_rows, :] + w[1] * l_s[1, rows, :] + w[2] * l_s[2, rows, :]
        o_ref[0, nat_rows, :] = (num / den).astype(o_ref.dtype)

    n_stage = 3
    chunks_per_res = Sq // rows_per
    first_d16 = len(tiles) - 16
    ready = {}
    for c in range(S // rows_per):
        last_tile = first_d16 + 4 * (c // chunks_per_res) + 3
        ready.setdefault(last_tile + (n_stage - 1) * SKEW + 1 + c % chunks_per_res, []).append(c)

    n_step = len(tiles) + (n_stage - 1) * SKEW
    assert len(reorders) <= S // BLK - 2
    for k in range(max(n_step, max(ready) + 1)):
        if k < len(reorders):
            reorders[k]()
        for stage in range(n_stage):
            if 0 <= k - stage * SKEW < len(tiles):
                next(tiles[k - stage * SKEW], None)
        for c in ready.get(k, []):
            combine(c)
        yield


def _mixers_kernel(q_ref, k_ref, v_ref, xa_ref, ga_ref, cw_ref, cb_ref, wg_ref, ba_ref,
                   bx_ref, lam_ref, g_ref, att_ref, ya_ref,
                   bias_s, x4_s, acc_s, m_s, l_s, xs, a_s, u_s, h_s, *, S, T):
    @pl.when(pl.program_id(0) == 0)
    def _():
        xs[:, 0:(CONV_WIDTH - 1) * SUBLANES, :] = jnp.zeros(
            (xs.shape[0], (CONV_WIDTH - 1) * SUBLANES, LANES), F32)
        h_s[...] = jnp.zeros(h_s.shape, F32)

    attn = _attn_stages(q_ref, k_ref, v_ref, att_ref, bias_s, x4_s, acc_s, m_s, l_s, S=S)
    rec = _rec_stages(xa_ref, ga_ref, cw_ref, cb_ref, wg_ref, ba_ref, bx_ref, lam_ref,
                      g_ref, ya_ref, xs, a_s, u_s, h_s, T=T)
    step = 0
    for _ in attn:
        if step % 2 == 1:
            next(rec, None)
        step += 1
    for _ in rec:
        pass


def _mixers_call(q, k, v, xa, ga, cw, cb, wg, ba, bx, lam, g):
    B, S, C = q.shape
    assert B == SUBLANES and xa.shape == (B, S, D_REC)
    n_pair = C // LANES
    n_step = B * n_pair
    T = S // n_step
    head = pl.BlockSpec((1, S, LANES), lambda s: (s // n_pair, 0, s % n_pair))
    seq = pl.BlockSpec((B, T, D_REC), lambda s: (0, s, 0))
    vec = pl.BlockSpec((1, D_REC), lambda s: (0, 0))
    n_slab = D_REC // LANES
    gates = pl.BlockSpec((n_slab, LANES, 2 * LANES), lambda s: (0, 0, 0))
    return pl.pallas_call(
        functools.partial(_mixers_kernel, S=S, T=T),
        grid=(n_step,),
        in_specs=[head, head, head, seq, seq,
                  pl.BlockSpec((CONV_WIDTH, D_REC), lambda s: (0, 0)), vec, gates,
                  vec, vec, vec, vec],
        out_specs=[head, seq],
        out_shape=[jax.ShapeDtypeStruct((B, S, C), F32), jax.ShapeDtypeStruct((B, S, D_REC), BF16)],
        scratch_shapes=[pltpu.VMEM((2 * BLK, 2 * BLK), BF16),
                        pltpu.VMEM((3, S, LANES), F32),
                        pltpu.VMEM((3, S, LANES), F32),
                        pltpu.VMEM((3, S, LANES), F32),
                        pltpu.VMEM((3, S, LANES), F32),
                        pltpu.VMEM((n_slab, (T + CONV_WIDTH - 1) * B, LANES), F32),
                        pltpu.VMEM((n_slab, T * B, LANES), F32),
                        pltpu.VMEM((n_slab, T * B, LANES), F32),
                        pltpu.VMEM((n_slab, B, LANES), F32)],
        compiler_params=pltpu.CompilerParams(
            dimension_semantics=("arbitrary",), vmem_limit_bytes=VMEM_LIMIT),
        name="mixers",
    )(q, k, v, xa, ga, cw, cb.reshape(1, D_REC), wg, ba.reshape(1, D_REC),
      bx.reshape(1, D_REC), lam.reshape(1, D_REC), g.reshape(1, D_REC))


def _outproj_rows(rows, x_ref, ya_ref, att_ref, gb_ref, mod_ref, natt_ref, npost_ref, w1_ref,
                  w2_ref, o_ref):
    ya = ya_ref[0, rows, :]
    yb = _rms(att_ref[0, rows, :] * _silu(gb_ref[0, rows, :].astype(F32)), natt_ref[...]).astype(BF16)
    yield
    mix = (jnp.dot(ya, w1_ref[...], preferred_element_type=F32)
           + jnp.dot(yb, w2_ref[...], preferred_element_type=F32))
    yield
    gate = mod_ref[0, :, 2 * D_MODEL:3 * D_MODEL]
    o_ref[0, rows, :] = x_ref[0, rows, :] + gate * _rms(mix, npost_ref[...])


def _outproj_kernel(x_ref, *refs):
    ts = x_ref.shape[1]
    _skewed([_outproj_rows(pl.ds(r0, SUB_ROWS), x_ref, *refs) for r0 in range(0, ts, SUB_ROWS)], 3)


def _outproj_call(x, ya, att, gb, mod, natt, npost, w1, w2, ts):
    B, S, D = x.shape
    seq = lambda b, i: (b, i, 0)
    const = lambda b, i: (0, 0)
    half = pl.BlockSpec((1, ts, D_REC), seq)
    return pl.pallas_call(
        _outproj_kernel,
        grid=(B, S // ts),
        in_specs=[pl.BlockSpec((1, ts, D), seq), half, half, half,
                  pl.BlockSpec((1, 1, 3 * D), lambda b, i: (b, 0, 0)),
                  pl.BlockSpec((1, D_ATT), const),
                  pl.BlockSpec((1, D), const),
                  pl.BlockSpec((D_REC, D), const),
                  pl.BlockSpec((D_ATT, D), const)],
        out_specs=pl.BlockSpec((1, ts, D), seq),
        out_shape=jax.ShapeDtypeStruct((B, S, D), F32),
        compiler_params=pltpu.CompilerParams(
            dimension_semantics=("parallel", "parallel"), vmem_limit_bytes=VMEM_LIMIT),
        name="outproj",
    )(x, ya, att, gb, mod.reshape(B, 1, 3 * D), natt.reshape(1, D_ATT), npost.reshape(1, D), w1, w2)


def _gate_weights(w_a, w_x):
    nb, n, _ = w_a.shape
    per = LANES // n
    eye = jnp.eye(per, dtype=w_a.dtype)

    def lane_blocks(w):
        w = w.reshape(nb // per, per, n, n)
        return jnp.einsum('jhik,hg->jhigk', w, eye).reshape(nb // per, LANES, LANES)

    return (0.5 * jnp.concatenate([lane_blocks(w_a), lane_blocks(w_x)], axis=-1)).astype(BF16)


def kernel(x, c, positions, w_ada, b_ada, norm_pre, norm_post, w_in, conv_w, conv_b, w_rg_a, b_rg_a,
           w_rg_x, b_rg_x, lru_lambda, norm_rec, norm_att, w_out):
    depth = w_in.shape[0]
    inv_freq = ROPE_THETA ** (-jnp.arange(HALF, dtype=F32) / HALF)
    invf = jnp.tile(inv_freq, LANES // HALF).reshape(1, LANES)
    for l in range(depth):
        mod = _mod_call(c, w_ada[l], b_ada[l])
        xa, ga, q, k, v, gb = _inproj_call(x, mod, norm_pre[l], positions, invf,
                                           w_in[l].astype(BF16), ts=512)
        att, ya = _mixers_call(q, k, v, xa, ga, conv_w[l], conv_b[l],
                               _gate_weights(w_rg_a[l], w_rg_x[l]),
                               b_rg_a[l], b_rg_x[l], lru_lambda[l], norm_rec[l])
        x = _outproj_call(x, ya, att, gb, mod, norm_att[l], norm_post[l],
                          w_out[l, :D_REC].astype(BF16), w_out[l, D_REC:].astype(BF16), ts=1024)
    return x
```

```python
import functools

import jax
import jax.numpy as jnp
from jax import lax
from jax.experimental import pallas as pl
from jax.experimental.pallas import tpu as pltpu

F32 = jnp.float32
BF16 = jnp.bfloat16

D_MODEL = 1024
D_REC = 512
D_ATT = 512
N_LRU_BLOCKS = 8
LRU_C = 8.0
CONV_WIDTH = 4
HEAD_DIM = 64
HALF = HEAD_DIM // 2
ROPE_THETA = 10000.0
NORM_EPS = 1e-6
NEG_INF = -1e30
D_IN_PROJ = 2 * D_REC + 4 * D_ATT

LANES = 128
SUBLANES = 8
BLK = 128
DILATIONS = (1, 4, 16)
SKEW = 1

VMEM_LIMIT = 56 * 1024 * 1024
Q_SCALE = HEAD_DIM ** -0.5 * 1.4426950408889634


def _sigmoid(x):
    return 0.5 * jnp.tanh(0.5 * x) + 0.5


def _silu(x):
    h = 0.5 * x
    return h + h * jnp.tanh(h)


def _rms(x, g):
    return x * lax.rsqrt(jnp.mean(x * x, axis=-1, keepdims=True) + NORM_EPS) * g


def _mod_kernel(c_ref, w_ref, b_ref, o_ref):
    o_ref[...] = jnp.dot(_silu(c_ref[...]).astype(BF16), w_ref[...].astype(BF16),
                         preferred_element_type=F32) + b_ref[...]


def _mod_call(c, w, b):
    B, D = c.shape
    N = w.shape[1]
    tn = 1024
    return pl.pallas_call(
        _mod_kernel,
        grid=(N // tn,),
        in_specs=[pl.BlockSpec((B, D), lambda j: (0, 0)),
                  pl.BlockSpec((D, tn), lambda j: (0, j)),
                  pl.BlockSpec((1, tn), lambda j: (0, j))],
        out_specs=pl.BlockSpec((B, tn), lambda j: (0, j)),
        out_shape=jax.ShapeDtypeStruct((B, N), F32),
        compiler_params=pltpu.CompilerParams(vmem_limit_bytes=VMEM_LIMIT),
        name="mod",
    )(c, w, b.reshape(1, N))


def _skewed(stages, n_stage):
    for k in range(len(stages) + n_stage - 1):
        for stage in range(n_stage):
            if 0 <= k - stage < len(stages):
                next(stages[k - stage], None)


SUB_ROWS = 256


def _inproj_chunks(r0, x_ref, mod_ref, g_ref, pos_ref, invf_ref, w_ref, *out_refs):
    rows = pl.ds(r0, SUB_ROWS)
    cache = {}

    def normed():
        if "h" not in cache:
            shift = mod_ref[0, :, 0:D_MODEL]
            scale = mod_ref[0, :, D_MODEL:2 * D_MODEL]
            cache["h"] = (_rms(x_ref[0, rows, :], g_ref[...]) * (1.0 + scale) + shift).astype(BF16)
        return cache["h"]

    def rotary():
        if "rot" not in cache:
            n_grp = LANES // HALF
            blk = SUB_ROWS // n_grp
            pos = pos_ref[pl.ds(pl.program_id(0), 1), rows].astype(F32)
            pos = jnp.concatenate([jnp.broadcast_to(pos[:, c:c + LANES], (LANES, LANES)).T
                                   for c in range(0, SUB_ROWS, LANES)], axis=0)
            lane = lax.broadcasted_iota(jnp.int32, (1, LANES), 1)
            grp = lane // HALF
            packed = pos[0:blk]
            for a in range(1, n_grp):
                packed = jnp.where(grp == a, pos[a * blk:(a + 1) * blk], packed)
            ang = packed * invf_ref[...]

            def spread(t):
                rolled = [t] + [pltpu.roll(t, HALF * k, 1) for k in range(1, n_grp)]
                blocks = []
                for a in range(n_grp):
                    out = rolled[-a % n_grp]
                    for b in range(1, n_grp):
                        out = jnp.where(grp == b, rolled[(b - a) % n_grp], out)
                    blocks.append(out)
                return jnp.concatenate(blocks, axis=0)

            first = (lane % HEAD_DIM) < HALF
            sin = spread(jnp.sin(ang))
            cache["rot"] = (spread(jnp.cos(ang)), jnp.where(first, -sin, sin), first)
        return cache["rot"]

    def rope(t, scale):
        cos, sin_signed, first = rotary()
        cols = []
        for j in range(D_ATT // LANES):
            tj = t[:, j * LANES:(j + 1) * LANES]
            partner = jnp.where(first, pltpu.roll(tj, LANES - HALF, 1), pltpu.roll(tj, HALF, 1))
            cols.append(tj * cos + partner * sin_signed)
        out = jnp.concatenate(cols, axis=1)
        return out if scale is None else out * scale

    epilogues = [None, None, lambda t: rope(t, Q_SCALE), lambda t: rope(t, None), None, None]

    def item(c, o_ref, epilogue):
        t = jnp.dot(normed(), w_ref[:, c * D_REC:(c + 1) * D_REC], preferred_element_type=F32)
        yield
        o_ref[0, rows, :] = (t if epilogue is None else epilogue(t)).astype(o_ref.dtype)

    return [item(c, o_ref, ep) for c, (o_ref, ep) in enumerate(zip(out_refs, epilogues))]


def _inproj_kernel(x_ref, *refs):
    ts = x_ref.shape[1]
    items = [it for r0 in range(0, ts, SUB_ROWS) for it in _inproj_chunks(r0, x_ref, *refs)]
    _skewed(items, 2)


def _inproj_call(x, mod, g, pos, invf, w, ts):
    B, S, D = x.shape
    seq = lambda b, i: (b, i, 0)
    const = lambda b, i: (0, 0)
    half = pl.BlockSpec((1, ts, D_REC), seq)
    half_shape = jax.ShapeDtypeStruct((B, S, D_REC), F32)
    return pl.pallas_call(
        _inproj_kernel,
        grid=(B, S // ts),
        in_specs=[pl.BlockSpec((1, ts, D), seq),
                  pl.BlockSpec((1, 1, 3 * D), lambda b, i: (b, 0, 0)),
                  pl.BlockSpec((1, D), const),
                  pl.BlockSpec((B, ts), lambda b, i: (0, i)),
                  pl.BlockSpec((1, LANES), const),
                  pl.BlockSpec((D, D_IN_PROJ), const)],
        out_specs=[half] * 6,
        out_shape=[half_shape] * 5 + [jax.ShapeDtypeStruct((B, S, D_ATT), BF16)],
        compiler_params=pltpu.CompilerParams(
            dimension_semantics=("parallel", "parallel"), vmem_limit_bytes=VMEM_LIMIT),
        name="inproj",
    )(x, mod.reshape(B, 1, 3 * D), g.reshape(1, D), pos, invf, w)


def _rec_stages(xa_ref, ga_ref, cw_ref, cb_ref, wg_ref, ba_ref, bx_ref, lam_ref, g_ref,
                o_ref, xs, a_s, u_s, h_s, *, T):
    B = SUBLANES
    tail = (CONV_WIDTH - 1) * B
    n_slab = D_REC // LANES
    slab = lambda j: slice(j * LANES, (j + 1) * LANES)

    for b in range(B):
        for j in range(n_slab):
            xs[j, pl.ds(tail + b, T, stride=B), :] = xa_ref[b, :, slab(j)]
        yield

    z = -lam_ref[...]
    half_rate = (-0.5 * LRU_C) * (jnp.maximum(z, 0.0) + jnp.log1p(jnp.exp(-jnp.abs(z))))
    half_ba = 0.5 * ba_ref[...]
    half_bx = 0.5 * bx_ref[...]
    chunk = 16 * B
    for r0 in range(0, T * B, chunk):
        for j in range(n_slab):
            xc = sum(cw_ref[k:k + 1, slab(j)] * xs[j, k * B + r0:k * B + r0 + chunk, :]
                     for k in range(CONV_WIDTH)) + cb_ref[:, slab(j)]
            pre = jnp.dot(xc.astype(BF16), wg_ref[j], preferred_element_type=F32)
            ta = jnp.tanh(pre[:, 0:LANES] + half_ba[:, slab(j)])
            ig = 0.5 * jnp.tanh(pre[:, LANES:] + half_bx[:, slab(j)]) + 0.5
            log_a = ta * half_rate[:, slab(j)] + half_rate[:, slab(j)]
            a = jnp.exp(log_a)
            a_s[j, r0:r0 + chunk, :] = a
            u_s[j, r0:r0 + chunk, :] = jnp.sqrt(-jnp.tanh(log_a) * (1.0 + a * a)) * (ig * xc)
        yield
    xs[:, 0:tail, :] = xs[:, T * B:T * B + tail, :]

    h = h_s[...]
    for t in range(T):
        rows = pl.ds(t * B, B)
        h = a_s[:, rows, :] * h + u_s[:, rows, :]
        u_s[:, rows, :] = h
        if t % 8 == 7:
            yield
    h_s[...] = h

    for b in range(B):
        hb = jnp.concatenate([u_s[j, pl.ds(b, T, stride=B), :] for j in range(n_slab)], axis=-1)
        o_ref[b] = _rms(hb * _silu(ga_ref[b].astype(F32)), g_ref[...]).astype(o_ref.dtype)
        yield


def _attn_stages(q_ref, k_ref, v_ref, o_ref, bias_s, x4_s, acc_s, m_s, l_s, *, S):
    qi = lax.broadcasted_iota(jnp.int32, (2 * BLK, 2 * BLK), 0) % BLK
    ki = lax.broadcasted_iota(jnp.int32, (2 * BLK, 2 * BLK), 1)
    dist = qi + BLK - ki
    bias_s[...] = jnp.where((dist >= 0) & (dist <= BLK), 0.0, NEG_INF).astype(bias_s.dtype)
    lane = lax.broadcasted_iota(jnp.int32, (1, LANES), 1)
    head_a = lane < HEAD_DIM
    ind_a = jnp.where(head_a, 1.0, 0.0).astype(BF16)
    ind_b = jnp.where(head_a, 0.0, 1.0).astype(BF16)

    Sq = S // 4

    def reorder(a, ref, r):
        x4_s[a, r * Sq:(r + 1) * Sq, :] = ref[0, pl.ds(r, Sq, stride=4), :]

    reorders = [functools.partial(reorder, a, ref, r)
                for a, ref in enumerate((q_ref, k_ref, v_ref)) for r in range(4)]

    natural = tuple((lambda rows, ref=ref: ref[0, rows, :]) for ref in (q_ref, k_ref, v_ref))
    mod4 = tuple((lambda rows, a=a: x4_s[a, rows, :]) for a in range(3))

    def tile(p, src, q_rows, k_rows, nk):
        qt = src[0](q_rows)
        kt = src[1](k_rows).astype(BF16)
        vt = src[2](k_rows).astype(BF16)
        q2 = jnp.concatenate([jnp.where(head_a, qt, 0.0), jnp.where(head_a, 0.0, qt)], axis=0)
        s = lax.dot_general(q2.astype(BF16), kt, (((1,), (1,)), ((), ())),
                            preferred_element_type=F32)
        yield
        sb = s.astype(BF16) + bias_s[:, 2 * BLK - nk:2 * BLK]
        m = jnp.max(sb, axis=-1, keepdims=True)
        e = jnp.exp2(sb - m)
        m = m.astype(F32)
        m_s[p, q_rows, :] = jnp.where(head_a, m[0:BLK], m[BLK:])
        yield
        v_a = jnp.concatenate([vt * ind_a, jnp.broadcast_to(ind_a, (nk, LANES))], axis=1)
        v_b = jnp.concatenate([vt * ind_b, jnp.broadcast_to(ind_b, (nk, LANES))], axis=1)
        r = jnp.dot(jnp.concatenate([e[0:BLK], e[BLK:]], axis=1), jnp.concatenate([v_a, v_b], axis=0),
                    preferred_element_type=F32)
        acc_s[p, q_rows, :] = r[:, 0:LANES]
        l_s[p, q_rows, :] = r[:, LANES:]

    def band_tile(p, src, q0, first):
        q_rows = pl.ds(q0, BLK)
        if first:
            return tile(p, src, q_rows, q_rows, BLK)
        return tile(p, src, q_rows, pl.ds(q0 - BLK, 2 * BLK), 2 * BLK)

    n_blk = Sq // BLK
    tiles = [band_tile(0, natural, n * BLK, n == 0) for n in range(S // BLK)]
    tiles += [band_tile(1, mod4, r * Sq + n * BLK, n == 0) for r in range(4) for n in range(n_blk)]
    for r in range(4):
        for e in range(4):
            rows = pl.ds(r * Sq + e, BLK, stride=4)
            tiles.append(tile(2, mod4, rows, rows, BLK))

    rows_per = 256

    def combine(c):
        rows = pl.ds(c * rows_per, rows_per)
        nat_rows = pl.ds(c // 2 + (c % 2) * (4 * rows_per), rows_per, stride=4)
        m = [m_s[0, nat_rows, :], m_s[1, rows, :], m_s[2, rows, :]]
        mx = jnp.maximum(jnp.maximum(m[0], m[1]), m[2])
        w = [jnp.exp2(mp - mx) for mp in m]
        num = w[0] * acc_s[0, nat_rows, :] + w[1] * acc_s[1, rows, :] + w[2] * acc_s[2, rows, :]
        den = w[0] * l_s[0, nat_rows, :] + w[1] * l_s[1, rows, :] + w[2] * l_s[2, rows, :]
        o_ref[0, nat_rows, :] = (num / den).astype(o_ref.dtype)

    n_stage = 3
    chunks_per_res = Sq // rows_per
    first_d16 = len(tiles) - 16
    ready = {}
    for c in range(S // rows_per):
        last_tile = first_d16 + 4 * (c // chunks_per_res) + 3
        ready.setdefault(last_tile + (n_stage - 1) * SKEW + 1 + c % chunks_per_res, []).append(c)

    n_step = len(tiles) + (n_stage - 1) * SKEW
    assert len(reorders) <= S // BLK - 2
    for k in range(max(n_step, max(ready) + 1)):
        if k < len(reorders):
            reorders[k]()
        for stage in range(n_stage):
            if 0 <= k - stage * SKEW < len(tiles):
                next(tiles[k - stage * SKEW], None)
        for c in ready.get(k, []):
            combine(c)
        yield


def _mixers_kernel(q_ref, k_ref, v_ref, xa_ref, ga_ref, cw_ref, cb_ref, wg_ref, ba_ref,
                   bx_ref, lam_ref, g_ref, att_ref, ya_ref,
                   bias_s, x4_s, acc_s, m_s, l_s, xs, a_s, u_s, h_s, *, S, T):
    @pl.when(pl.program_id(0) == 0)
    def _():
        xs[:, 0:(CONV_WIDTH - 1) * SUBLANES, :] = jnp.zeros(
            (xs.shape[0], (CONV_WIDTH - 1) * SUBLANES, LANES), F32)
        h_s[...] = jnp.zeros(h_s.shape, F32)

    attn = _attn_stages(q_ref, k_ref, v_ref, att_ref, bias_s, x4_s, acc_s, m_s, l_s, S=S)
    rec = _rec_stages(xa_ref, ga_ref, cw_ref, cb_ref, wg_ref, ba_ref, bx_ref, lam_ref,
                      g_ref, ya_ref, xs, a_s, u_s, h_s, T=T)
    step = 0
    for _ in attn:
        if step % 2 == 1:
            next(rec, None)
        step += 1
    for _ in rec:
        pass


def _mixers_call(q, k, v, xa, ga, cw, cb, wg, ba, bx, lam, g):
    B, S, C = q.shape
    assert B == SUBLANES and xa.shape == (B, S, D_REC)
    n_pair = C // LANES
    n_step = B * n_pair
    T = S // n_step
    head = pl.BlockSpec((1, S, LANES), lambda s: (s // n_pair, 0, s % n_pair))
    seq = pl.BlockSpec((B, T, D_REC), lambda s: (0, s, 0))
    vec = pl.BlockSpec((1, D_REC), lambda s: (0, 0))
    n_slab = D_REC // LANES
    gates = pl.BlockSpec((n_slab, LANES, 2 * LANES), lambda s: (0, 0, 0))
    return pl.pallas_call(
        functools.partial(_mixers_kernel, S=S, T=T),
        grid=(n_step,),
        in_specs=[head, head, head, seq, seq,
                  pl.BlockSpec((CONV_WIDTH, D_REC), lambda s: (0, 0)), vec, gates,
                  vec, vec, vec, vec],
        out_specs=[head, seq],
        out_shape=[jax.ShapeDtypeStruct((B, S, C), F32), jax.ShapeDtypeStruct((B, S, D_REC), BF16)],
        scratch_shapes=[pltpu.VMEM((2 * BLK, 2 * BLK), BF16),
                        pltpu.VMEM((3, S, LANES), F32),
                        pltpu.VMEM((3, S, LANES), F32),
                        pltpu.VMEM((3, S, LANES), F32),
                        pltpu.VMEM((3, S, LANES), F32),
                        pltpu.VMEM((n_slab, (T + CONV_WIDTH - 1) * B, LANES), F32),
                        pltpu.VMEM((n_slab, T * B, LANES), F32),
                        pltpu.VMEM((n_slab, T * B, LANES), F32),
                        pltpu.VMEM((n_slab, B, LANES), F32)],
        compiler_params=pltpu.CompilerParams(
            dimension_semantics=("arbitrary",), vmem_limit_bytes=VMEM_LIMIT),
        name="mixers",
    )(q, k, v, xa, ga, cw, cb.reshape(1, D_REC), wg, ba.reshape(1, D_REC),
      bx.reshape(1, D_REC), lam.reshape(1, D_REC), g.reshape(1, D_REC))


def _outproj_rows(rows, x_ref, ya_ref, att_ref, gb_ref, mod_ref, natt_ref, npost_ref, w1_ref,
                  w2_ref, o_ref):
    ya = ya_ref[0, rows, :]
    yb = _rms(att_ref[0, rows, :] * _silu(gb_ref[0, rows, :].astype(F32)), natt_ref[...]).astype(BF16)
    yield
    mix = (jnp.dot(ya, w1_ref[...], preferred_element_type=F32)
           + jnp.dot(yb, w2_ref[...], preferred_element_type=F32))
    yield
    gate = mod_ref[0, :, 2 * D_MODEL:3 * D_MODEL]
    o_ref[0, rows, :] = x_ref[0, rows, :] + gate * _rms(mix, npost_ref[...])


def _outproj_kernel(x_ref, *refs):
    ts = x_ref.shape[1]
    _skewed([_outproj_rows(pl.ds(r0, SUB_ROWS), x_ref, *refs) for r0 in range(0, ts, SUB_ROWS)], 3)


def _outproj_call(x, ya, att, gb, mod, natt, npost, w1, w2, ts):
    B, S, D = x.shape
    seq = lambda b, i: (b, i, 0)
    const = lambda b, i: (0, 0)
    half = pl.BlockSpec((1, ts, D_REC), seq)
    return pl.pallas_call(
        _outproj_kernel,
        grid=(B, S // ts),
        in_specs=[pl.BlockSpec((1, ts, D), seq), half, half, half,
                  pl.BlockSpec((1, 1, 3 * D), lambda b, i: (b, 0, 0)),
                  pl.BlockSpec((1, D_ATT), const),
                  pl.BlockSpec((1, D), const),
                  pl.BlockSpec((D_REC, D), const),
                  pl.BlockSpec((D_ATT, D), const)],
        out_specs=pl.BlockSpec((1, ts, D), seq),
        out_shape=jax.ShapeDtypeStruct((B, S, D), F32),
        compiler_params=pltpu.CompilerParams(
            dimension_semantics=("parallel", "parallel"), vmem_limit_bytes=VMEM_LIMIT),
        name="outproj",
    )(x, ya, att, gb, mod.reshape(B, 1, 3 * D), natt.reshape(1, D_ATT), npost.reshape(1, D), w1, w2)


def _gate_weights(w_a, w_x):
    nb, n, _ = w_a.shape
    per = LANES // n
    eye = jnp.eye(per, dtype=w_a.dtype)

    def lane_blocks(w):
        w = w.reshape(nb // per, per, n, n)
        return jnp.einsum('jhik,hg->jhigk', w, eye).reshape(nb // per, LANES, LANES)

    return (0.5 * jnp.concatenate([lane_blocks(w_a), lane_blocks(w_x)], axis=-1)).astype(BF16)


def kernel(x, c, positions, w_ada, b_ada, norm_pre, norm_post, w_in, conv_w, conv_b, w_rg_a, b_rg_a,
           w_rg_x, b_rg_x, lru_lambda, norm_rec, norm_att, w_out):
    depth = w_in.shape[0]
    inv_freq = ROPE_THETA ** (-jnp.arange(HALF, dtype=F32) / HALF)
    invf = jnp.tile(inv_freq, LANES // HALF).reshape(1, LANES)
    for l in range(depth):
        mod = _mod_call(c, w_ada[l], b_ada[l])
        xa, ga, q, k, v, gb = _inproj_call(x, mod, norm_pre[l], positions, invf,
                                           w_in[l].astype(BF16), ts=512)
        att, ya = _mixers_call(q, k, v, xa, ga, conv_w[l], conv_b[l],
                               _gate_weights(w_rg_a[l], w_rg_x[l]),
                               b_rg_a[l], b_rg_x[l], lru_lambda[l], norm_rec[l])
        x = _outproj_call(x, ya, att, gb, mod, norm_att[l], norm_post[l],
                          w_out[l, :D_REC].astype(BF16), w_out[l, D_REC:].astype(BF16), ts=1024)
    return x
```

```python
import functools

import jax
import jax.numpy as jnp
from jax import lax
from jax.experimental import pallas as pl
from jax.experimental.pallas import tpu as pltpu

F32 = jnp.float32
BF16 = jnp.bfloat16

D_MODEL = 1024
D_REC = 512
D_ATT = 512
N_LRU_BLOCKS = 8
LRU_C = 8.0
CONV_WIDTH = 4
HEAD_DIM = 64
HALF = HEAD_DIM // 2
ROPE_THETA = 10000.0
NORM_EPS = 1e-6
NEG_INF = -1e30
D_IN_PROJ = 2 * D_REC + 4 * D_ATT

LANES = 128
SUBLANES = 8
BLK = 128
DILATIONS = (1, 4, 16)
SKEW = 1

VMEM_LIMIT = 56 * 1024 * 1024
Q_SCALE = HEAD_DIM ** -0.5 * 1.4426950408889634


def _sigmoid(x):
    return 0.5 * jnp.tanh(0.5 * x) + 0.5


def _silu(x):
    h = 0.5 * x
    return h + h * jnp.tanh(h)


def _rms(x, g):
    return x * lax.rsqrt(jnp.mean(x * x, axis=-1, keepdims=True) + NORM_EPS) * g


def _mod_kernel(c_ref, w_ref, b_ref, o_ref):
    o_ref[...] = jnp.dot(_silu(c_ref[...]).astype(BF16), w_ref[...].astype(BF16),
                         preferred_element_type=F32) + b_ref[...]


def _mod_call(c, w, b):
    B, D = c.shape
    N = w.shape[1]
    tn = 1024
    return pl.pallas_call(
        _mod_kernel,
        grid=(N // tn,),
        in_specs=[pl.BlockSpec((B, D), lambda j: (0, 0)),
                  pl.BlockSpec((D, tn), lambda j: (0, j)),
                  pl.BlockSpec((1, tn), lambda j: (0, j))],
        out_specs=pl.BlockSpec((B, tn), lambda j: (0, j)),
        out_shape=jax.ShapeDtypeStruct((B, N), F32),
        compiler_params=pltpu.CompilerParams(vmem_limit_bytes=VMEM_LIMIT),
        name="mod",
    )(c, w, b.reshape(1, N))


def _skewed(stages, n_stage):
    for k in range(len(stages) + n_stage - 1):
        for stage in range(n_stage):
            if 0 <= k - stage < len(stages):
                next(stages[k - stage], None)


SUB_ROWS = 256


def _inproj_chunks(r0, x_ref, mod_ref, g_ref, pos_ref, invf_ref, w_ref, *out_refs):
    rows = pl.ds(r0, SUB_ROWS)
    cache = {}

    def normed():
        if "h" not in cache:
            batch = pl.ds(pl.program_id(0), 1)
            shift = mod_ref[batch, 0:D_MODEL]
            scale = mod_ref[batch, D_MODEL:2 * D_MODEL]
            cache["h"] = (_rms(x_ref[0, rows, :], g_ref[...]) * (1.0 + scale) + shift).astype(BF16)
        return cache["h"]

    def rotary():
        if "rot" not in cache:
            n_grp = LANES // HALF
            blk = SUB_ROWS // n_grp
            pos = pos_ref[pl.ds(pl.program_id(0), 1), rows].astype(F32)
            pos = jnp.concatenate([jnp.broadcast_to(pos[:, c:c + LANES], (LANES, LANES)).T
                                   for c in range(0, SUB_ROWS, LANES)], axis=0)
            lane = lax.broadcasted_iota(jnp.int32, (1, LANES), 1)
            grp = lane // HALF
            packed = pos[0:blk]
            for a in range(1, n_grp):
                packed = jnp.where(grp == a, pos[a * blk:(a + 1) * blk], packed)
            ang = packed * invf_ref[...]

            def spread(t):
                rolled = [t] + [pltpu.roll(t, HALF * k, 1) for k in range(1, n_grp)]
                blocks = []
                for a in range(n_grp):
                    out = rolled[-a % n_grp]
                    for b in range(1, n_grp):
                        out = jnp.where(grp == b, rolled[(b - a) % n_grp], out)
                    blocks.append(out)
                return jnp.concatenate(blocks, axis=0)

            first = (lane % HEAD_DIM) < HALF
            sin = spread(jnp.sin(ang))
            cache["rot"] = (spread(jnp.cos(ang)), jnp.where(first, -sin, sin), first)
        return cache["rot"]

    def rope(t, scale):
        cos, sin_signed, first = rotary()
        cols = []
        for j in range(D_ATT // LANES):
            tj = t[:, j * LANES:(j + 1) * LANES]
            partner = jnp.where(first, pltpu.roll(tj, LANES - HALF, 1), pltpu.roll(tj, HALF, 1))
            cols.append(tj * cos + partner * sin_signed)
        out = jnp.concatenate(cols, axis=1)
        return out if scale is None else out * scale

    epilogues = [None, None, lambda t: rope(t, Q_SCALE), lambda t: rope(t, None), None, None]

    def item(c, o_ref, epilogue):
        t = jnp.dot(normed(), w_ref[:, c * D_REC:(c + 1) * D_REC], preferred_element_type=F32)
        yield
        o_ref[0, rows, :] = (t if epilogue is None else epilogue(t)).astype(o_ref.dtype)

    return [item(c, o_ref, ep) for c, (o_ref, ep) in enumerate(zip(out_refs, epilogues))]


def _inproj_kernel(x_ref, mod_ref, g_ref, pos_ref, invf_ref, w_ref, *refs):
    *out_refs, w_bf = refs
    ts = x_ref.shape[1]

    @pl.when((pl.program_id(0) == 0) & (pl.program_id(1) == 0))
    def _():
        for c in range(0, D_IN_PROJ, D_REC):
            w_bf[:, c:c + D_REC] = w_ref[:, c:c + D_REC].astype(BF16)

    items = [it for r0 in range(0, ts, SUB_ROWS)
             for it in _inproj_chunks(r0, x_ref, mod_ref, g_ref, pos_ref, invf_ref, w_bf, *out_refs)]
    _skewed(items, 2)


def _inproj_call(x, mod, g, pos, invf, w, ts):
    B, S, D = x.shape
    seq = lambda b, i: (b, i, 0)
    const = lambda b, i: (0, 0)
    half = pl.BlockSpec((1, ts, D_REC), seq)
    half_shape = jax.ShapeDtypeStruct((B, S, D_REC), F32)
    return pl.pallas_call(
        _inproj_kernel,
        grid=(B, S // ts),
        in_specs=[pl.BlockSpec((1, ts, D), seq),
                  pl.BlockSpec((B, 3 * D), const),
                  pl.BlockSpec((1, D), const),
                  pl.BlockSpec((B, ts), lambda b, i: (0, i)),
                  pl.BlockSpec((1, LANES), const),
                  pl.BlockSpec((D, D_IN_PROJ), const, pipeline_mode=pl.Buffered(1))],
        out_specs=[half] * 6,
        out_shape=[half_shape] * 5 + [jax.ShapeDtypeStruct((B, S, D_ATT), BF16)],
        scratch_shapes=[pltpu.VMEM((D, D_IN_PROJ), BF16)],
        compiler_params=pltpu.CompilerParams(
            dimension_semantics=("arbitrary", "arbitrary"), vmem_limit_bytes=VMEM_LIMIT),
        name="inproj",
    )(x, mod, g.reshape(1, D), pos, invf, w)


def _rec_stages(xa_ref, ga_ref, cw_ref, cb_ref, wg_ref, ba_ref, bx_ref, lam_ref, g_ref,
                o_ref, xs, a_s, u_s, h_s, *, T):
    B = SUBLANES
    tail = (CONV_WIDTH - 1) * B
    n_slab = D_REC // LANES
    slab = lambda j: slice(j * LANES, (j + 1) * LANES)

    for b in range(B):
        for j in range(n_slab):
            xs[j, pl.ds(tail + b, T, stride=B), :] = xa_ref[b, :, slab(j)]
        yield

    z = -lam_ref[...]
    half_rate = (-0.5 * LRU_C) * (jnp.maximum(z, 0.0) + jnp.log1p(jnp.exp(-jnp.abs(z))))
    half_ba = 0.5 * ba_ref[...]
    half_bx = 0.5 * bx_ref[...]
    chunk = 16 * B
    for r0 in range(0, T * B, chunk):
        for j in range(n_slab):
            xc = sum(cw_ref[k:k + 1, slab(j)] * xs[j, k * B + r0:k * B + r0 + chunk, :]
                     for k in range(CONV_WIDTH)) + cb_ref[:, slab(j)]
            pre = jnp.dot(xc.astype(BF16), wg_ref[j], preferred_element_type=F32)
            ta = jnp.tanh(pre[:, 0:LANES] + half_ba[:, slab(j)])
            ig = 0.5 * jnp.tanh(pre[:, LANES:] + half_bx[:, slab(j)]) + 0.5
            log_a = ta * half_rate[:, slab(j)] + half_rate[:, slab(j)]
            a = jnp.exp(log_a)
            a_s[j, r0:r0 + chunk, :] = a
            u_s[j, r0:r0 + chunk, :] = jnp.sqrt(-jnp.tanh(log_a) * (1.0 + a * a)) * (ig * xc)
        yield
    xs[:, 0:tail, :] = xs[:, T * B:T * B + tail, :]

    h = h_s[...]
    for t in range(T):
        rows = pl.ds(t * B, B)
        h = a_s[:, rows, :] * h + u_s[:, rows, :]
        u_s[:, rows, :] = h
        if t % 8 == 7:
            yield
    h_s[...] = h

    for b in range(B):
        hb = jnp.concatenate([u_s[j, pl.ds(b, T, stride=B), :] for j in range(n_slab)], axis=-1)
        o_ref[b] = _rms(hb * _silu(ga_ref[b].astype(F32)), g_ref[...]).astype(o_ref.dtype)
        yield


def _attn_stages(q_ref, k_ref, v_ref, o_ref, bias_s, x4_s, acc_s, m_s, l_s, *, S):
    qi = lax.broadcasted_iota(jnp.int32, (2 * BLK, 2 * BLK), 0) % BLK
    ki = lax.broadcasted_iota(jnp.int32, (2 * BLK, 2 * BLK), 1)
    dist = qi + BLK - ki
    bias_s[...] = jnp.where((dist >= 0) & (dist <= BLK), 0.0, NEG_INF).astype(bias_s.dtype)
    lane = lax.broadcasted_iota(jnp.int32, (1, LANES), 1)
    head_a = lane < HEAD_DIM
    ind_a = jnp.where(head_a, 1.0, 0.0).astype(BF16)
    ind_b = jnp.where(head_a, 0.0, 1.0).astype(BF16)

    Sq = S // 4

    def reorder(a, ref, r):
        x4_s[a, r * Sq:(r + 1) * Sq, :] = ref[0, pl.ds(r, Sq, stride=4), :]

    reorders = [functools.partial(reorder, a, ref, r)
                for a, ref in enumerate((q_ref, k_ref, v_ref)) for r in range(4)]

    natural = tuple((lambda rows, ref=ref: ref[0, rows, :]) for ref in (q_ref, k_ref, v_ref))
    mod4 = tuple((lambda rows, a=a: x4_s[a, rows, :]) for a in range(3))

    def tile(p, src, q_rows, k_rows, nk):
        qt = src[0](q_rows)
        kt = src[1](k_rows).astype(BF16)
        vt = src[2](k_rows).astype(BF16)
        q2 = jnp.concatenate([jnp.where(head_a, qt, 0.0), jnp.where(head_a, 0.0, qt)], axis=0)
        s = lax.dot_general(q2.astype(BF16), kt, (((1,), (1,)), ((), ())),
                            preferred_element_type=F32)
        yield
        sb = s.astype(BF16) + bias_s[:, 2 * BLK - nk:2 * BLK]
        m = jnp.max(sb, axis=-1, keepdims=True)
        e = jnp.exp2(sb - m)
        m = m.astype(F32)
        m_s[p, q_rows, :] = jnp.where(head_a, m[0:BLK], m[BLK:])
        yield
        v_a = jnp.concatenate([vt * ind_a, jnp.broadcast_to(ind_a, (nk, LANES))], axis=1)
        v_b = jnp.concatenate([vt * ind_b, jnp.broadcast_to(ind_b, (nk, LANES))], axis=1)
        r = jnp.dot(jnp.concatenate([e[0:BLK], e[BLK:]], axis=1), jnp.concatenate([v_a, v_b], axis=0),
                    preferred_element_type=F32)
        acc_s[p, q_rows, :] = r[:, 0:LANES]
        l_s[p, q_rows, :] = r[:, LANES:]

    def band_tile(p, src, q0, first):
        q_rows = pl.ds(q0, BLK)
        if first:
            return tile(p, src, q_rows, q_rows, BLK)
        return tile(p, src, q_rows, pl.ds(q0 - BLK, 2 * BLK), 2 * BLK)

    n_blk = Sq // BLK
    tiles = [band_tile(0, natural, n * BLK, n == 0) for n in range(S // BLK)]
    tiles += [band_tile(1, mod4, r * Sq + n * BLK, n == 0) for r in range(4) for n in range(n_blk)]
    for r in range(4):
        for e in range(4):
            rows = pl.ds(r * Sq + e, BLK, stride=4)
            tiles.append(tile(2, mod4, rows, rows, BLK))

    rows_per = 256

    def combine(c):
        rows = pl.ds(c * rows_per, rows_per)
        nat_rows = pl.ds(c // 2 + (c % 2) * (4 * rows_per), rows_per, stride=4)
        m = [m_s[0, nat_rows, :], m_s[1, rows, :], m_s[2, rows, :]]
        mx = jnp.maximum(jnp.maximum(m[0], m[1]), m[2])
        w = [jnp.exp2(mp - mx) for mp in m]
        num = w[0] * acc_s[0, nat_rows, :] + w[1] * acc_s[1, rows, :] + w[2] * acc_s[2, rows, :]
        den = w[0] * l_s[0, nat_rows, :] + w[1] * l_s[1, rows, :] + w[2] * l_s[2, rows, :]
        o_ref[0, nat_rows, :] = (num / den).astype(o_ref.dtype)

    n_stage = 3
    chunks_per_res = Sq // rows_per
    first_d16 = len(tiles) - 16
    ready = {}
    for c in range(S // rows_per):
        last_tile = first_d16 + 4 * (c // chunks_per_res) + 3
        ready.setdefault(last_tile + (n_stage - 1) * SKEW + 1 + c % chunks_per_res, []).append(c)

    n_step = len(tiles) + (n_stage - 1) * SKEW
    assert len(reorders) <= S // BLK - 2
    for k in range(max(n_step, max(ready) + 1)):
        if k < len(reorders):
            reorders[k]()
        for stage in range(n_stage):
            if 0 <= k - stage * SKEW < len(tiles):
                next(tiles[k - stage * SKEW], None)
        for c in ready.get(k, []):
            combine(c)
        yield


def _mixers_kernel(q_ref, k_ref, v_ref, xa_ref, ga_ref, cw_ref, cb_ref, wg_ref, ba_ref,
                   bx_ref, lam_ref, g_ref, att_ref, ya_ref,
                   bias_s, x4_s, acc_s, m_s, l_s, xs, a_s, u_s, h_s, *, S, T):
    @pl.when(pl.program_id(0) == 0)
    def _():
        xs[:, 0:(CONV_WIDTH - 1) * SUBLANES, :] = jnp.zeros(
            (xs.shape[0], (CONV_WIDTH - 1) * SUBLANES, LANES), F32)
        h_s[...] = jnp.zeros(h_s.shape, F32)

    attn = _attn_stages(q_ref, k_ref, v_ref, att_ref, bias_s, x4_s, acc_s, m_s, l_s, S=S)
    rec = _rec_stages(xa_ref, ga_ref, cw_ref, cb_ref, wg_ref, ba_ref, bx_ref, lam_ref,
                      g_ref, ya_ref, xs, a_s, u_s, h_s, T=T)
    step = 0
    for _ in attn:
        if step % 2 == 1:
            next(rec, None)
        step += 1
    for _ in rec:
        pass


def _mixers_call(q, k, v, xa, ga, cw, cb, wg, ba, bx, lam, g):
    B, S, C = q.shape
    assert B == SUBLANES and xa.shape == (B, S, D_REC)
    n_pair = C // LANES
    n_step = B * n_pair
    T = S // n_step
    head = pl.BlockSpec((1, S, LANES), lambda s: (s // n_pair, 0, s % n_pair))
    seq = pl.BlockSpec((B, T, D_REC), lambda s: (0, s, 0))
    vec = pl.BlockSpec((1, D_REC), lambda s: (0, 0))
    n_slab = D_REC // LANES
    gates = pl.BlockSpec((n_slab, LANES, 2 * LANES), lambda s: (0, 0, 0))
    return pl.pallas_call(
        functools.partial(_mixers_kernel, S=S, T=T),
        grid=(n_step,),
        in_specs=[head, head, head, seq, seq,
                  pl.BlockSpec((CONV_WIDTH, D_REC), lambda s: (0, 0)), vec, gates,
                  vec, vec, vec, vec],
        out_specs=[head, seq],
        out_shape=[jax.ShapeDtypeStruct((B, S, C), F32), jax.ShapeDtypeStruct((B, S, D_REC), BF16)],
        scratch_shapes=[pltpu.VMEM((2 * BLK, 2 * BLK), BF16),
                        pltpu.VMEM((3, S, LANES), F32),
                        pltpu.VMEM((3, S, LANES), F32),
                        pltpu.VMEM((3, S, LANES), F32),
                        pltpu.VMEM((3, S, LANES), F32),
                        pltpu.VMEM((n_slab, (T + CONV_WIDTH - 1) * B, LANES), F32),
                        pltpu.VMEM((n_slab, T * B, LANES), F32),
                        pltpu.VMEM((n_slab, T * B, LANES), F32),
                        pltpu.VMEM((n_slab, B, LANES), F32)],
        compiler_params=pltpu.CompilerParams(
            dimension_semantics=("arbitrary",), vmem_limit_bytes=VMEM_LIMIT),
        name="mixers",
    )(q, k, v, xa, ga, cw, cb.reshape(1, D_REC), wg, ba.reshape(1, D_REC),
      bx.reshape(1, D_REC), lam.reshape(1, D_REC), g.reshape(1, D_REC))


def _outproj_rows(rows, x_ref, ya_ref, att_ref, gb_ref, mod_ref, natt_ref, npost_ref, w_ref,
                  o_ref):
    ya = ya_ref[0, rows, :]
    yb = _rms(att_ref[0, rows, :] * _silu(gb_ref[0, rows, :].astype(F32)), natt_ref[...]).astype(BF16)
    yield
    mix = jnp.dot(jnp.concatenate([ya, yb], axis=1), w_ref[...], preferred_element_type=F32)
    yield
    gate = mod_ref[pl.ds(pl.program_id(0), 1), 2 * D_MODEL:3 * D_MODEL]
    o_ref[0, rows, :] = x_ref[0, rows, :] + gate * _rms(mix, npost_ref[...])


def _outproj_kernel(x_ref, *refs):
    ts = x_ref.shape[1]
    _skewed([_outproj_rows(pl.ds(r0, SUB_ROWS), x_ref, *refs) for r0 in range(0, ts, SUB_ROWS)], 3)


def _outproj_call(x, ya, att, gb, mod, natt, npost, w, ts):
    B, S, D = x.shape
    seq = lambda b, i: (b, i, 0)
    const = lambda b, i: (0, 0)
    half = pl.BlockSpec((1, ts, D_REC), seq)
    return pl.pallas_call(
        _outproj_kernel,
        grid=(B, S // ts),
        in_specs=[pl.BlockSpec((1, ts, D), seq), half, half, half,
                  pl.BlockSpec((B, 3 * D), const),
                  pl.BlockSpec((1, D_ATT), const),
                  pl.BlockSpec((1, D), const),
                  pl.BlockSpec((D_REC + D_ATT, D), const)],
        out_specs=pl.BlockSpec((1, ts, D), seq),
        out_shape=jax.ShapeDtypeStruct((B, S, D), F32),
        compiler_params=pltpu.CompilerParams(
            dimension_semantics=("parallel", "parallel"), vmem_limit_bytes=VMEM_LIMIT),
        name="outproj",
    )(x, ya, att, gb, mod, natt.reshape(1, D_ATT), npost.reshape(1, D), w)


def _gate_weights(w_a, w_x):
    nb, n, _ = w_a.shape
    per = LANES // n
    eye = jnp.eye(per, dtype=w_a.dtype)

    def lane_blocks(w):
        w = w.reshape(nb // per, per, n, n)
        return jnp.einsum('jhik,hg->jhigk', w, eye).reshape(nb // per, LANES, LANES)

    return (0.5 * jnp.concatenate([lane_blocks(w_a), lane_blocks(w_x)], axis=-1)).astype(BF16)


def kernel(x, c, positions, w_ada, b_ada, norm_pre, norm_post, w_in, conv_w, conv_b, w_rg_a, b_rg_a,
           w_rg_x, b_rg_x, lru_lambda, norm_rec, norm_att, w_out):
    depth = w_in.shape[0]
    inv_freq = ROPE_THETA ** (-jnp.arange(HALF, dtype=F32) / HALF)
    invf = jnp.tile(inv_freq, LANES // HALF).reshape(1, LANES)
    for l in range(depth):
        mod = _mod_call(c, w_ada[l], b_ada[l])
        xa, ga, q, k, v, gb = _inproj_call(x, mod, norm_pre[l], positions, invf,
                                           w_in[l], ts=1024)
        att, ya = _mixers_call(q, k, v, xa, ga, conv_w[l], conv_b[l],
                               _gate_weights(w_rg_a[l], w_rg_x[l]),
                               b_rg_a[l], b_rg_x[l], lru_lambda[l], norm_rec[l])
        x = _outproj_call(x, ya, att, gb, mod, norm_att[l], norm_post[l],
                          w_out[l].astype(BF16), ts=1024)
    return x
```

```python
import functools

import jax
import jax.numpy as jnp
from jax import lax
from jax.experimental import pallas as pl
from jax.experimental.pallas import tpu as pltpu

F32 = jnp.float32
BF16 = jnp.bfloat16

D_MODEL = 1024
D_REC = 512
D_ATT = 512
N_LRU_BLOCKS = 8
LRU_C = 8.0
CONV_WIDTH = 4
HEAD_DIM = 64
HALF = HEAD_DIM // 2
ROPE_THETA = 10000.0
NORM_EPS = 1e-6
NEG_INF = -1e30
D_IN_PROJ = 2 * D_REC + 4 * D_ATT

LANES = 128
SUBLANES = 8
BLK = 128
DILATIONS = (1, 4, 16)
SKEW = 1

VMEM_LIMIT = 56 * 1024 * 1024
Q_SCALE = HEAD_DIM ** -0.5 * 1.4426950408889634


def _sigmoid(x):
    return 0.5 * jnp.tanh(0.5 * x) + 0.5


def _silu(x):
    h = 0.5 * x
    return h + h * jnp.tanh(h)


def _rms(x, g):
    return x * lax.rsqrt(jnp.mean(x * x, axis=-1, keepdims=True) + NORM_EPS) * g


def _mod_kernel(c_ref, w_ref, b_ref, o_ref):
    o_ref[...] = jnp.dot(_silu(c_ref[...]).astype(BF16), w_ref[...].astype(BF16),
                         preferred_element_type=F32) + b_ref[...]


def _mod_call(c, w, b):
    B, D = c.shape
    N = w.shape[1]
    tn = 1024
    return pl.pallas_call(
        _mod_kernel,
        grid=(N // tn,),
        in_specs=[pl.BlockSpec((B, D), lambda j: (0, 0)),
                  pl.BlockSpec((D, tn), lambda j: (0, j)),
                  pl.BlockSpec((1, tn), lambda j: (0, j))],
        out_specs=pl.BlockSpec((B, tn), lambda j: (0, j)),
        out_shape=jax.ShapeDtypeStruct((B, N), F32),
        compiler_params=pltpu.CompilerParams(vmem_limit_bytes=VMEM_LIMIT),
        name="mod",
    )(c, w, b.reshape(1, N))


def _skewed(stages, n_stage):
    for k in range(len(stages) + n_stage - 1):
        for stage in range(n_stage):
            if 0 <= k - stage < len(stages):
                next(stages[k - stage], None)


SUB_ROWS = 256


def _inproj_chunks(r0, x_ref, mod_ref, g_ref, pos_ref, invf_ref, w_ref, *out_refs):
    rows = pl.ds(r0, SUB_ROWS)
    cache = {}

    def normed():
        if "h" not in cache:
            batch = pl.ds(pl.program_id(0), 1)
            shift = mod_ref[batch, 0:D_MODEL]
            scale = mod_ref[batch, D_MODEL:2 * D_MODEL]
            cache["h"] = (_rms(x_ref[0, rows, :], g_ref[...]) * (1.0 + scale) + shift).astype(BF16)
        return cache["h"]

    def rotary():
        if "rot" not in cache:
            n_grp = LANES // HALF
            blk = SUB_ROWS // n_grp
            pos = pos_ref[pl.ds(pl.program_id(0), 1), rows].astype(F32)
            pos = jnp.concatenate([jnp.broadcast_to(pos[:, c:c + LANES], (LANES, LANES)).T
                                   for c in range(0, SUB_ROWS, LANES)], axis=0)
            lane = lax.broadcasted_iota(jnp.int32, (1, LANES), 1)
            grp = lane // HALF
            packed = pos[0:blk]
            for a in range(1, n_grp):
                packed = jnp.where(grp == a, pos[a * blk:(a + 1) * blk], packed)
            ang = packed * invf_ref[...]

            def spread(t):
                rolled = [t] + [pltpu.roll(t, HALF * k, 1) for k in range(1, n_grp)]
                blocks = []
                for a in range(n_grp):
                    out = rolled[-a % n_grp]
                    for b in range(1, n_grp):
                        out = jnp.where(grp == b, rolled[(b - a) % n_grp], out)
                    blocks.append(out)
                return jnp.concatenate(blocks, axis=0)

            first = (lane % HEAD_DIM) < HALF
            sin = spread(jnp.sin(ang))
            cache["rot"] = (spread(jnp.cos(ang)), jnp.where(first, -sin, sin), first)
        return cache["rot"]

    def rope(t, scale):
        cos, sin_signed, first = rotary()
        cols = []
        for j in range(D_ATT // LANES):
            tj = t[:, j * LANES:(j + 1) * LANES]
            partner = jnp.where(first, pltpu.roll(tj, LANES - HALF, 1), pltpu.roll(tj, HALF, 1))
            cols.append(tj * cos + partner * sin_signed)
        out = jnp.concatenate(cols, axis=1)
        return out if scale is None else out * scale

    epilogues = [None, None, lambda t: rope(t, Q_SCALE), lambda t: rope(t, None), None, None]

    def item(c, o_ref, epilogue):
        t = jnp.dot(normed(), w_ref[:, c * D_REC:(c + 1) * D_REC], preferred_element_type=F32)
        yield
        o_ref[0, rows, :] = (t if epilogue is None else epilogue(t)).astype(o_ref.dtype)

    return [item(c, o_ref, ep) for c, (o_ref, ep) in enumerate(zip(out_refs, epilogues))]


def _inproj_kernel(x_ref, mod_ref, g_ref, pos_ref, invf_ref, w_ref, *refs):
    *out_refs, w_bf = refs
    ts = x_ref.shape[1]

    @pl.when((pl.program_id(0) == 0) & (pl.program_id(1) == 0))
    def _():
        for c in range(0, D_IN_PROJ, D_REC):
            w_bf[:, c:c + D_REC] = w_ref[:, c:c + D_REC].astype(BF16)

    items = [it for r0 in range(0, ts, SUB_ROWS)
             for it in _inproj_chunks(r0, x_ref, mod_ref, g_ref, pos_ref, invf_ref, w_bf, *out_refs)]
    _skewed(items, 2)


def _inproj_call(x, mod, g, pos, invf, w, ts):
    B, S, D = x.shape
    seq = lambda b, i: (b, i, 0)
    const = lambda b, i: (0, 0)
    half = pl.BlockSpec((1, ts, D_REC), seq)
    half_shape = jax.ShapeDtypeStruct((B, S, D_REC), F32)
    return pl.pallas_call(
        _inproj_kernel,
        grid=(B, S // ts),
        in_specs=[pl.BlockSpec((1, ts, D), seq),
                  pl.BlockSpec((B, 3 * D), const),
                  pl.BlockSpec((1, D), const),
                  pl.BlockSpec((B, ts), lambda b, i: (0, i)),
                  pl.BlockSpec((1, LANES), const),
                  pl.BlockSpec((D, D_IN_PROJ), const, pipeline_mode=pl.Buffered(1))],
        out_specs=[half] * 6,
        out_shape=[half_shape] * 5 + [jax.ShapeDtypeStruct((B, S, D_ATT), BF16)],
        scratch_shapes=[pltpu.VMEM((D, D_IN_PROJ), BF16)],
        compiler_params=pltpu.CompilerParams(
            dimension_semantics=("arbitrary", "arbitrary"), vmem_limit_bytes=VMEM_LIMIT),
        name="inproj",
    )(x, mod, g.reshape(1, D), pos, invf, w)


def _rec_stages(xa_ref, ga_ref, cw_ref, cb_ref, wg_ref, ba_ref, bx_ref, lam_ref, g_ref,
                o_ref, xs, a_s, u_s, h_s, *, T):
    B = SUBLANES
    tail = (CONV_WIDTH - 1) * B
    n_slab = D_REC // LANES
    slab = lambda j: slice(j * LANES, (j + 1) * LANES)

    for b in range(B):
        for j in range(n_slab):
            xs[j, pl.ds(tail + b, T, stride=B), :] = xa_ref[b, :, slab(j)]
        yield

    z = -lam_ref[...]
    half_rate = (-0.5 * LRU_C) * (jnp.maximum(z, 0.0) + jnp.log1p(jnp.exp(-jnp.abs(z))))
    half_ba = 0.5 * ba_ref[...]
    half_bx = 0.5 * bx_ref[...]
    chunk = 16 * B
    for r0 in range(0, T * B, chunk):
        for j in range(n_slab):
            xc = sum(cw_ref[k:k + 1, slab(j)] * xs[j, k * B + r0:k * B + r0 + chunk, :]
                     for k in range(CONV_WIDTH)) + cb_ref[:, slab(j)]
            pre = jnp.dot(xc.astype(BF16), wg_ref[j], preferred_element_type=F32)
            ta = jnp.tanh(pre[:, 0:LANES] + half_ba[:, slab(j)])
            ig = 0.5 * jnp.tanh(pre[:, LANES:] + half_bx[:, slab(j)]) + 0.5
            log_a = ta * half_rate[:, slab(j)] + half_rate[:, slab(j)]
            a = jnp.exp(log_a)
            a_s[j, r0:r0 + chunk, :] = a
            u_s[j, r0:r0 + chunk, :] = jnp.sqrt(-jnp.tanh(log_a) * (1.0 + a * a)) * (ig * xc)
        yield
    xs[:, 0:tail, :] = xs[:, T * B:T * B + tail, :]

    h = h_s[...]
    for t in range(T):
        rows = pl.ds(t * B, B)
        h = a_s[:, rows, :] * h + u_s[:, rows, :]
        u_s[:, rows, :] = h
        if t % 8 == 7:
            yield
    h_s[...] = h

    for b in range(B):
        hb = jnp.concatenate([u_s[j, pl.ds(b, T, stride=B), :] for j in range(n_slab)], axis=-1)
        o_ref[b] = _rms(hb * _silu(ga_ref[b].astype(F32)), g_ref[...]).astype(o_ref.dtype)
        yield


def _attn_stages(q_ref, k_ref, v_ref, o_ref, bias_s, x4_s, acc_s, m_s, l_s, *, S):
    qi = lax.broadcasted_iota(jnp.int32, (2 * BLK, 2 * BLK), 0) % BLK
    ki = lax.broadcasted_iota(jnp.int32, (2 * BLK, 2 * BLK), 1)
    dist = qi + BLK - ki
    bias_s[...] = jnp.where((dist >= 0) & (dist <= BLK), 0.0, NEG_INF).astype(bias_s.dtype)
    lane = lax.broadcasted_iota(jnp.int32, (1, LANES), 1)
    head_a = lane < HEAD_DIM
    ind_a = jnp.where(head_a, 1.0, 0.0).astype(BF16)
    ind_b = jnp.where(head_a, 0.0, 1.0).astype(BF16)

    Sq = S // 4

    def reorder(a, ref, r):
        x4_s[a, r * Sq:(r + 1) * Sq, :] = ref[0, pl.ds(r, Sq, stride=4), :]

    reorders = [functools.partial(reorder, a, ref, r)
                for a, ref in enumerate((q_ref, k_ref, v_ref)) for r in range(4)]

    natural = tuple((lambda rows, ref=ref: ref[0, rows, :]) for ref in (q_ref, k_ref, v_ref))
    mod4 = tuple((lambda rows, a=a: x4_s[a, rows, :]) for a in range(3))

    def tile(p, src, q_rows, k_rows, nk):
        qt = src[0](q_rows)
        kt = src[1](k_rows).astype(BF16)
        vt = src[2](k_rows).astype(BF16)
        q2 = jnp.concatenate([jnp.where(head_a, qt, 0.0), jnp.where(head_a, 0.0, qt)], axis=0)
        s = lax.dot_general(q2.astype(BF16), kt, (((1,), (1,)), ((), ())),
                            preferred_element_type=F32)
        yield
        sb = s.astype(BF16) + bias_s[:, 2 * BLK - nk:2 * BLK]
        m = jnp.max(sb, axis=-1, keepdims=True)
        e = jnp.exp2(sb - m)
        m = m.astype(F32)
        m_s[p, q_rows, :] = jnp.where(head_a, m[0:BLK], m[BLK:])
        yield
        v_a = jnp.concatenate([vt * ind_a, jnp.broadcast_to(ind_a, (nk, LANES))], axis=1)
        v_b = jnp.concatenate([vt * ind_b, jnp.broadcast_to(ind_b, (nk, LANES))], axis=1)
        r = jnp.dot(jnp.concatenate([e[0:BLK], e[BLK:]], axis=1), jnp.concatenate([v_a, v_b], axis=0),
                    preferred_element_type=F32)
        acc_s[p, q_rows, :] = r[:, 0:LANES]
        l_s[p, q_rows, :] = r[:, LANES:]

    def band_tile(p, src, q0, first):
        q_rows = pl.ds(q0, BLK)
        if first:
            return tile(p, src, q_rows, q_rows, BLK)
        return tile(p, src, q_rows, pl.ds(q0 - BLK, 2 * BLK), 2 * BLK)

    n_blk = Sq // BLK
    tiles = [band_tile(0, natural, n * BLK, n == 0) for n in range(S // BLK)]
    tiles += [band_tile(1, mod4, r * Sq + n * BLK, n == 0) for r in range(4) for n in range(n_blk)]
    for r in range(4):
        for e in range(4):
            rows = pl.ds(r * Sq + e, BLK, stride=4)
            tiles.append(tile(2, mod4, rows, rows, BLK))

    rows_per = 256

    def combine(c):
        rows = pl.ds(c * rows_per, rows_per)
        nat_rows = pl.ds(c // 2 + (c % 2) * (4 * rows_per), rows_per, stride=4)
        m = [m_s[0, nat_rows, :], m_s[1, rows, :], m_s[2, rows, :]]
        mx = jnp.maximum(jnp.maximum(m[0], m[1]), m[2])
        w = [jnp.exp2(mp - mx) for mp in m]
        num = w[0] * acc_s[0, nat_rows, :] + w[1] * acc_s[1, rows, :] + w[2] * acc_s[2, rows, :]
        den = w[0] * l_s[0, nat_rows, :] + w[1] * l_s[1, rows, :] + w[2] * l_s[2, rows, :]
        o_ref[0, rows, :] = (num / den).astype(o_ref.dtype)

    n_stage = 3
    chunks_per_res = Sq // rows_per
    first_d16 = len(tiles) - 16
    ready = {}
    for c in range(S // rows_per):
        last_tile = first_d16 + 4 * (c // chunks_per_res) + 3
        ready.setdefault(last_tile + (n_stage - 1) * SKEW + 1 + c % chunks_per_res, []).append(c)

    n_step = len(tiles) + (n_stage - 1) * SKEW
    assert len(reorders) <= S // BLK - 2
    for k in range(max(n_step, max(ready) + 1)):
        if k < len(reorders):
            reorders[k]()
        for stage in range(n_stage):
            if 0 <= k - stage * SKEW < len(tiles):
                next(tiles[k - stage * SKEW], None)
        for c in ready.get(k, []):
            combine(c)
        yield


def _mixers_kernel(q_ref, k_ref, v_ref, xa_ref, ga_ref, cw_ref, cb_ref, wg_ref, ba_ref,
                   bx_ref, lam_ref, g_ref, att_ref, ya_ref,
                   bias_s, x4_s, acc_s, m_s, l_s, xs, a_s, u_s, h_s, *, S, T):
    @pl.when(pl.program_id(0) == 0)
    def _():
        xs[:, 0:(CONV_WIDTH - 1) * SUBLANES, :] = jnp.zeros(
            (xs.shape[0], (CONV_WIDTH - 1) * SUBLANES, LANES), F32)
        h_s[...] = jnp.zeros(h_s.shape, F32)

    attn = _attn_stages(q_ref, k_ref, v_ref, att_ref, bias_s, x4_s, acc_s, m_s, l_s, S=S)
    rec = _rec_stages(xa_ref, ga_ref, cw_ref, cb_ref, wg_ref, ba_ref, bx_ref, lam_ref,
                      g_ref, ya_ref, xs, a_s, u_s, h_s, T=T)
    step = 0
    for _ in attn:
        if step % 2 == 1:
            next(rec, None)
        step += 1
    for _ in rec:
        pass


def _mixers_call(q, k, v, xa, ga, cw, cb, wg, ba, bx, lam, g):
    B, S, C = q.shape
    assert B == SUBLANES and xa.shape == (B, S, D_REC)
    n_pair = C // LANES
    n_step = B * n_pair
    T = S // n_step
    head = pl.BlockSpec((1, S, LANES), lambda s: (s // n_pair, 0, s % n_pair))
    seq = pl.BlockSpec((B, T, D_REC), lambda s: (0, s, 0))
    vec = pl.BlockSpec((1, D_REC), lambda s: (0, 0))
    n_slab = D_REC // LANES
    gates = pl.BlockSpec((n_slab, LANES, 2 * LANES), lambda s: (0, 0, 0))
    return pl.pallas_call(
        functools.partial(_mixers_kernel, S=S, T=T),
        grid=(n_step,),
        in_specs=[head, head, head, seq, seq,
                  pl.BlockSpec((CONV_WIDTH, D_REC), lambda s: (0, 0)), vec, gates,
                  vec, vec, vec, vec],
        out_specs=[head, seq],
        out_shape=[jax.ShapeDtypeStruct((B, S, C), BF16), jax.ShapeDtypeStruct((B, S, D_REC), BF16)],
        scratch_shapes=[pltpu.VMEM((2 * BLK, 2 * BLK), BF16),
                        pltpu.VMEM((3, S, LANES), F32),
                        pltpu.VMEM((3, S, LANES), F32),
                        pltpu.VMEM((3, S, LANES), F32),
                        pltpu.VMEM((3, S, LANES), F32),
                        pltpu.VMEM((n_slab, (T + CONV_WIDTH - 1) * B, LANES), F32),
                        pltpu.VMEM((n_slab, T * B, LANES), F32),
                        pltpu.VMEM((n_slab, T * B, LANES), F32),
                        pltpu.VMEM((n_slab, B, LANES), F32)],
        compiler_params=pltpu.CompilerParams(
            dimension_semantics=("arbitrary",), vmem_limit_bytes=VMEM_LIMIT),
        name="mixers",
    )(q, k, v, xa, ga, cw, cb.reshape(1, D_REC), wg, ba.reshape(1, D_REC),
      bx.reshape(1, D_REC), lam.reshape(1, D_REC), g.reshape(1, D_REC))


def _outproj_rows(r0, x_ref, ya_ref, att_ref, gb_ref, mod_ref, natt_ref, npost_ref, w_ref,
                  o_ref, att_s):
    rows = pl.ds(r0, SUB_ROWS)
    n_slab = D_ATT // LANES
    for r in range(4):
        part = att_ref[0, r, r0 // 4:(r0 + SUB_ROWS) // 4, :].astype(F32)
        for j in range(n_slab):
            att_s[j, pl.ds(r0 + r, SUB_ROWS // 4, stride=4), :] = part[:, j * LANES:(j + 1) * LANES]
    att = jnp.concatenate([att_s[j, rows, :] for j in range(n_slab)], axis=1)
    ya = ya_ref[0, rows, :]
    yb = _rms(att * _silu(gb_ref[0, rows, :].astype(F32)), natt_ref[...]).astype(BF16)
    yield
    mix = jnp.dot(jnp.concatenate([ya, yb], axis=1), w_ref[...], preferred_element_type=F32)
    yield
    gate = mod_ref[pl.ds(pl.program_id(0), 1), 2 * D_MODEL:3 * D_MODEL]
    o_ref[0, rows, :] = x_ref[0, rows, :] + gate * _rms(mix, npost_ref[...])


def _outproj_kernel(x_ref, *refs):
    ts = x_ref.shape[1]
    _skewed([_outproj_rows(r0, x_ref, *refs) for r0 in range(0, ts, SUB_ROWS)], 3)


def _outproj_call(x, ya, att, gb, mod, natt, npost, w, ts):
    B, S, D = x.shape
    seq = lambda b, i: (b, i, 0)
    const = lambda b, i: (0, 0)
    half = pl.BlockSpec((1, ts, D_REC), seq)
    return pl.pallas_call(
        _outproj_kernel,
        grid=(B, S // ts),
        in_specs=[pl.BlockSpec((1, ts, D), seq), half,
                  pl.BlockSpec((1, 4, ts // 4, D_ATT), lambda b, i: (b, 0, i, 0)), half,
                  pl.BlockSpec((B, 3 * D), const),
                  pl.BlockSpec((1, D_ATT), const),
                  pl.BlockSpec((1, D), const),
                  pl.BlockSpec((D_REC + D_ATT, D), const)],
        out_specs=pl.BlockSpec((1, ts, D), seq),
        out_shape=jax.ShapeDtypeStruct((B, S, D), F32),
        scratch_shapes=[pltpu.VMEM((D_ATT // LANES, ts, LANES), F32)],
        compiler_params=pltpu.CompilerParams(
            dimension_semantics=("parallel", "parallel"), vmem_limit_bytes=VMEM_LIMIT),
        name="outproj",
    )(x, ya, att.reshape(B, 4, S // 4, D_ATT), gb, mod, natt.reshape(1, D_ATT), npost.reshape(1, D), w)


def _gate_weights(w_a, w_x):
    nb, n, _ = w_a.shape
    per = LANES // n
    eye = jnp.eye(per, dtype=w_a.dtype)

    def lane_blocks(w):
        w = w.reshape(nb // per, per, n, n)
        return jnp.einsum('jhik,hg->jhigk', w, eye).reshape(nb // per, LANES, LANES)

    return (0.5 * jnp.concatenate([lane_blocks(w_a), lane_blocks(w_x)], axis=-1)).astype(BF16)


def kernel(x, c, positions, w_ada, b_ada, norm_pre, norm_post, w_in, conv_w, conv_b, w_rg_a, b_rg_a,
           w_rg_x, b_rg_x, lru_lambda, norm_rec, norm_att, w_out):
    depth = w_in.shape[0]
    inv_freq = ROPE_THETA ** (-jnp.arange(HALF, dtype=F32) / HALF)
    invf = jnp.tile(inv_freq, LANES // HALF).reshape(1, LANES)
    for l in range(depth):
        mod = _mod_call(c, w_ada[l], b_ada[l])
        xa, ga, q, k, v, gb = _inproj_call(x, mod, norm_pre[l], positions, invf,
                                           w_in[l], ts=1024)
        att, ya = _mixers_call(q, k, v, xa, ga, conv_w[l], conv_b[l],
                               _gate_weights(w_rg_a[l], w_rg_x[l]),
                               b_rg_a[l], b_rg_x[l], lru_lambda[l], norm_rec[l])
        x = _outproj_call(x, ya, att, gb, mod, norm_att[l], norm_post[l],
                          w_out[l].astype(BF16), ts=1024)
    return x
```

```python
import functools

import jax
import jax.numpy as jnp
from jax import lax
from jax.experimental import pallas as pl
from jax.experimental.pallas import tpu as pltpu

F32 = jnp.float32
BF16 = jnp.bfloat16

D_MODEL = 1024
D_REC = 512
D_ATT = 512
N_LRU_BLOCKS = 8
LRU_C = 8.0
CONV_WIDTH = 4
HEAD_DIM = 64
HALF = HEAD_DIM // 2
ROPE_THETA = 10000.0
NORM_EPS = 1e-6
NEG_INF = -1e30
D_IN_PROJ = 2 * D_REC + 4 * D_ATT

LANES = 128
SUBLANES = 8
BLK = 128
DILATIONS = (1, 4, 16)
SKEW = 1

VMEM_LIMIT = 56 * 1024 * 1024
Q_SCALE = HEAD_DIM ** -0.5 * 1.4426950408889634


def _sigmoid(x):
    return 0.5 * jnp.tanh(0.5 * x) + 0.5


def _silu(x):
    h = 0.5 * x
    return h + h * jnp.tanh(h)


def _rms(x, g):
    return x * lax.rsqrt(jnp.mean(x * x, axis=-1, keepdims=True) + NORM_EPS) * g


def _mod_kernel(c_ref, w_ref, b_ref, o_ref):
    o_ref[...] = jnp.dot(_silu(c_ref[...]).astype(BF16), w_ref[...].astype(BF16),
                         preferred_element_type=F32) + b_ref[...]


def _mod_call(c, w, b):
    B, D = c.shape
    N = w.shape[1]
    tn = 1024
    return pl.pallas_call(
        _mod_kernel,
        grid=(N // tn,),
        in_specs=[pl.BlockSpec((B, D), lambda j: (0, 0)),
                  pl.BlockSpec((D, tn), lambda j: (0, j)),
                  pl.BlockSpec((1, tn), lambda j: (0, j))],
        out_specs=pl.BlockSpec((B, tn), lambda j: (0, j)),
        out_shape=jax.ShapeDtypeStruct((B, N), F32),
        compiler_params=pltpu.CompilerParams(vmem_limit_bytes=VMEM_LIMIT),
        name="mod",
    )(c, w, b.reshape(1, N))


def _skewed(stages, n_stage):
    for k in range(len(stages) + n_stage - 1):
        for stage in range(n_stage):
            if 0 <= k - stage < len(stages):
                next(stages[k - stage], None)


SUB_ROWS = 256
OUT_SUB_ROWS = 512


def _inproj_chunks(r0, x_ref, mod_ref, g_ref, pos_ref, invf_ref, w_ref, *out_refs):
    rows = pl.ds(r0, SUB_ROWS)
    cache = {}

    def normed():
        if "h" not in cache:
            batch = pl.ds(pl.program_id(0), 1)
            shift = mod_ref[batch, 0:D_MODEL]
            scale = mod_ref[batch, D_MODEL:2 * D_MODEL]
            cache["h"] = (_rms(x_ref[0, rows, :], g_ref[...]) * (1.0 + scale) + shift).astype(BF16)
        return cache["h"]

    def rotary():
        if "rot" not in cache:
            n_grp = LANES // HALF
            blk = SUB_ROWS // n_grp
            pos = pos_ref[pl.ds(pl.program_id(0), 1), rows].astype(F32)
            pos = jnp.concatenate([jnp.broadcast_to(pos[:, c:c + LANES], (LANES, LANES)).T
                                   for c in range(0, SUB_ROWS, LANES)], axis=0)
            lane = lax.broadcasted_iota(jnp.int32, (1, LANES), 1)
            grp = lane // HALF
            packed = pos[0:blk]
            for a in range(1, n_grp):
                packed = jnp.where(grp == a, pos[a * blk:(a + 1) * blk], packed)
            ang = packed * invf_ref[...]

            def spread(t):
                rolled = [t] + [pltpu.roll(t, HALF * k, 1) for k in range(1, n_grp)]
                blocks = []
                for a in range(n_grp):
                    out = rolled[-a % n_grp]
                    for b in range(1, n_grp):
                        out = jnp.where(grp == b, rolled[(b - a) % n_grp], out)
                    blocks.append(out)
                return jnp.concatenate(blocks, axis=0)

            first = (lane % HEAD_DIM) < HALF
            sin = spread(jnp.sin(ang))
            cache["rot"] = (spread(jnp.cos(ang)), jnp.where(first, -sin, sin), first)
        return cache["rot"]

    def rope(t, scale):
        cos, sin_signed, first = rotary()
        cols = []
        for j in range(D_ATT // LANES):
            tj = t[:, j * LANES:(j + 1) * LANES]
            partner = jnp.where(first, pltpu.roll(tj, LANES - HALF, 1), pltpu.roll(tj, HALF, 1))
            cols.append(tj * cos + partner * sin_signed)
        out = jnp.concatenate(cols, axis=1)
        return out if scale is None else out * scale

    epilogues = [None, None, lambda t: rope(t, Q_SCALE), lambda t: rope(t, None), None, None]

    def item(c, o_ref, epilogue):
        t = jnp.dot(normed(), w_ref[:, c * D_REC:(c + 1) * D_REC], preferred_element_type=F32)
        yield
        o_ref[0, rows, :] = (t if epilogue is None else epilogue(t)).astype(o_ref.dtype)

    return [item(c, o_ref, ep) for c, (o_ref, ep) in enumerate(zip(out_refs, epilogues))]


def _inproj_kernel(x_ref, mod_ref, g_ref, pos_ref, invf_ref, w_ref, *refs):
    *out_refs, w_bf = refs
    ts = x_ref.shape[1]

    @pl.when((pl.program_id(0) == 0) & (pl.program_id(1) == 0))
    def _():
        for c in range(0, D_IN_PROJ, D_REC):
            w_bf[:, c:c + D_REC] = w_ref[:, c:c + D_REC].astype(BF16)

    items = [it for r0 in range(0, ts, SUB_ROWS)
             for it in _inproj_chunks(r0, x_ref, mod_ref, g_ref, pos_ref, invf_ref, w_bf, *out_refs)]
    _skewed(items, 2)


def _inproj_call(x, mod, g, pos, invf, w, ts):
    B, S, D = x.shape
    seq = lambda b, i: (b, i, 0)
    const = lambda b, i: (0, 0)
    half = pl.BlockSpec((1, ts, D_REC), seq)
    half_shape = jax.ShapeDtypeStruct((B, S, D_REC), F32)
    return pl.pallas_call(
        _inproj_kernel,
        grid=(B, S // ts),
        in_specs=[pl.BlockSpec((1, ts, D), seq),
                  pl.BlockSpec((B, 3 * D), const),
                  pl.BlockSpec((1, D), const),
                  pl.BlockSpec((B, ts), lambda b, i: (0, i)),
                  pl.BlockSpec((1, LANES), const),
                  pl.BlockSpec((D, D_IN_PROJ), const, pipeline_mode=pl.Buffered(1))],
        out_specs=[half] * 6,
        out_shape=[half_shape] * 5 + [jax.ShapeDtypeStruct((B, S, D_ATT), BF16)],
        scratch_shapes=[pltpu.VMEM((D, D_IN_PROJ), BF16)],
        compiler_params=pltpu.CompilerParams(
            dimension_semantics=("arbitrary", "arbitrary"), vmem_limit_bytes=VMEM_LIMIT),
        name="inproj",
    )(x, mod, g.reshape(1, D), pos, invf, w)


def _rec_stages(xa_ref, ga_ref, cw_ref, cb_ref, wg_ref, ba_ref, bx_ref, lam_ref, g_ref,
                o_ref, xs, a_s, u_s, h_s, *, T):
    B = SUBLANES
    tail = (CONV_WIDTH - 1) * B
    n_slab = D_REC // LANES
    slab = lambda j: slice(j * LANES, (j + 1) * LANES)

    for b in range(B):
        for j in range(n_slab):
            xs[j, pl.ds(tail + b, T, stride=B), :] = xa_ref[b, :, slab(j)]
        yield

    z = -lam_ref[...]
    half_rate = (-0.5 * LRU_C) * (jnp.maximum(z, 0.0) + jnp.log1p(jnp.exp(-jnp.abs(z))))
    half_ba = 0.5 * ba_ref[...]
    half_bx = 0.5 * bx_ref[...]
    chunk = 16 * B
    for r0 in range(0, T * B, chunk):
        for j in range(n_slab):
            xc = sum(cw_ref[k:k + 1, slab(j)] * xs[j, k * B + r0:k * B + r0 + chunk, :]
                     for k in range(CONV_WIDTH)) + cb_ref[:, slab(j)]
            pre = jnp.dot(xc.astype(BF16), wg_ref[j], preferred_element_type=F32)
            ta = jnp.tanh(pre[:, 0:LANES] + half_ba[:, slab(j)])
            ig = 0.5 * jnp.tanh(pre[:, LANES:] + half_bx[:, slab(j)]) + 0.5
            log_a = ta * half_rate[:, slab(j)] + half_rate[:, slab(j)]
            a = jnp.exp(log_a)
            a_s[j, r0:r0 + chunk, :] = a
            u_s[j, r0:r0 + chunk, :] = jnp.sqrt(-jnp.tanh(log_a) * (1.0 + a * a)) * (ig * xc)
        yield
    xs[:, 0:tail, :] = xs[:, T * B:T * B + tail, :]

    h = h_s[...]
    for t in range(T):
        rows = pl.ds(t * B, B)
        h = a_s[:, rows, :] * h + u_s[:, rows, :]
        u_s[:, rows, :] = h
        if t % 8 == 7:
            yield
    h_s[...] = h

    for b in range(B):
        hb = jnp.concatenate([u_s[j, pl.ds(b, T, stride=B), :] for j in range(n_slab)], axis=-1)
        o_ref[b] = _rms(hb * _silu(ga_ref[b].astype(F32)), g_ref[...]).astype(o_ref.dtype)
        yield


def _attn_stages(q_ref, k_ref, v_ref, o_ref, bias_s, x4_s, acc_s, m_s, l_s, *, S):
    qi = lax.broadcasted_iota(jnp.int32, (2 * BLK, 2 * BLK), 0) % BLK
    ki = lax.broadcasted_iota(jnp.int32, (2 * BLK, 2 * BLK), 1)
    dist = qi + BLK - ki
    bias_s[...] = jnp.where((dist >= 0) & (dist <= BLK), 0.0, NEG_INF).astype(bias_s.dtype)
    lane = lax.broadcasted_iota(jnp.int32, (1, LANES), 1)
    head_a = lane < HEAD_DIM
    ind_a = jnp.where(head_a, 1.0, 0.0).astype(BF16)
    ind_b = jnp.where(head_a, 0.0, 1.0).astype(BF16)

    Sq = S // 4

    def reorder(a, ref, r):
        x4_s[a, r * Sq:(r + 1) * Sq, :] = ref[0, pl.ds(r, Sq, stride=4), :]

    reorders = [functools.partial(reorder, a, ref, r)
                for a, ref in enumerate((q_ref, k_ref, v_ref)) for r in range(4)]

    natural = tuple((lambda rows, ref=ref: ref[0, rows, :]) for ref in (q_ref, k_ref, v_ref))
    mod4 = tuple((lambda rows, a=a: x4_s[a, rows, :]) for a in range(3))

    def tile(p, src, q_rows, k_rows, nk):
        qt = src[0](q_rows)
        kt = src[1](k_rows).astype(BF16)
        vt = src[2](k_rows).astype(BF16)
        q2 = jnp.concatenate([jnp.where(head_a, qt, 0.0), jnp.where(head_a, 0.0, qt)], axis=0)
        s = lax.dot_general(q2.astype(BF16), kt, (((1,), (1,)), ((), ())),
                            preferred_element_type=F32)
        yield
        sb = s.astype(BF16) + bias_s[:, 2 * BLK - nk:2 * BLK]
        m = jnp.max(sb, axis=-1, keepdims=True)
        e = jnp.exp2(sb - m)
        m = m.astype(F32)
        m_s[p, q_rows, :] = jnp.where(head_a, m[0:BLK], m[BLK:])
        yield
        v_a = jnp.concatenate([vt * ind_a, jnp.broadcast_to(ind_a, (nk, LANES))], axis=1)
        v_b = jnp.concatenate([vt * ind_b, jnp.broadcast_to(ind_b, (nk, LANES))], axis=1)
        r = jnp.dot(jnp.concatenate([e[0:BLK], e[BLK:]], axis=1), jnp.concatenate([v_a, v_b], axis=0),
                    preferred_element_type=F32)
        acc_s[p, q_rows, :] = r[:, 0:LANES]
        l_s[p, q_rows, :] = r[:, LANES:]

    def band_tile(p, src, q0, first):
        q_rows = pl.ds(q0, BLK)
        if first:
            return tile(p, src, q_rows, q_rows, BLK)
        return tile(p, src, q_rows, pl.ds(q0 - BLK, 2 * BLK), 2 * BLK)

    n_blk = Sq // BLK
    tiles = [band_tile(0, natural, n * BLK, n == 0) for n in range(S // BLK)]
    tiles += [band_tile(1, mod4, r * Sq + n * BLK, n == 0) for r in range(4) for n in range(n_blk)]
    for r in range(4):
        for e in range(4):
            rows = pl.ds(r * Sq + e, BLK, stride=4)
            tiles.append(tile(2, mod4, rows, rows, BLK))

    rows_per = 256

    def combine(c):
        rows = pl.ds(c * rows_per, rows_per)
        nat_rows = pl.ds(c // 2 + (c % 2) * (4 * rows_per), rows_per, stride=4)
        m = [m_s[0, nat_rows, :], m_s[1, rows, :], m_s[2, rows, :]]
        mx = jnp.maximum(jnp.maximum(m[0], m[1]), m[2])
        w = [jnp.exp2(mp - mx) for mp in m]
        num = w[0] * acc_s[0, nat_rows, :] + w[1] * acc_s[1, rows, :] + w[2] * acc_s[2, rows, :]
        den = w[0] * l_s[0, nat_rows, :] + w[1] * l_s[1, rows, :] + w[2] * l_s[2, rows, :]
        o_ref[0, rows, :] = (num / den).astype(o_ref.dtype)

    n_stage = 3
    chunks_per_res = Sq // rows_per
    first_d16 = len(tiles) - 16
    ready = {}
    for c in range(S // rows_per):
        last_tile = first_d16 + 4 * (c // chunks_per_res) + 3
        ready.setdefault(last_tile + (n_stage - 1) * SKEW + 1 + c % chunks_per_res, []).append(c)

    n_step = len(tiles) + (n_stage - 1) * SKEW
    assert len(reorders) <= S // BLK - 2
    for k in range(max(n_step, max(ready) + 1)):
        if k < len(reorders):
            reorders[k]()
        for stage in range(n_stage):
            if 0 <= k - stage * SKEW < len(tiles):
                next(tiles[k - stage * SKEW], None)
        for c in ready.get(k, []):
            combine(c)
        yield


def _mixers_kernel(q_ref, k_ref, v_ref, xa_ref, ga_ref, cw_ref, cb_ref, wg_ref, ba_ref,
                   bx_ref, lam_ref, g_ref, att_ref, ya_ref,
                   bias_s, x4_s, acc_s, m_s, l_s, xs, a_s, u_s, h_s, *, S, T):
    @pl.when(pl.program_id(0) == 0)
    def _():
        xs[:, 0:(CONV_WIDTH - 1) * SUBLANES, :] = jnp.zeros(
            (xs.shape[0], (CONV_WIDTH - 1) * SUBLANES, LANES), F32)
        h_s[...] = jnp.zeros(h_s.shape, F32)

    attn = _attn_stages(q_ref, k_ref, v_ref, att_ref, bias_s, x4_s, acc_s, m_s, l_s, S=S)
    rec = _rec_stages(xa_ref, ga_ref, cw_ref, cb_ref, wg_ref, ba_ref, bx_ref, lam_ref,
                      g_ref, ya_ref, xs, a_s, u_s, h_s, T=T)
    step = 0
    for _ in attn:
        if step % 2 == 1:
            next(rec, None)
        step += 1
    for _ in rec:
        pass


def _mixers_call(q, k, v, xa, ga, cw, cb, wg, ba, bx, lam, g):
    B, S, C = q.shape
    assert B == SUBLANES and xa.shape == (B, S, D_REC)
    n_pair = C // LANES
    n_step = B * n_pair
    T = S // n_step
    head = pl.BlockSpec((1, S, LANES), lambda s: (s // n_pair, 0, s % n_pair))
    seq = pl.BlockSpec((B, T, D_REC), lambda s: (0, s, 0))
    vec = pl.BlockSpec((1, D_REC), lambda s: (0, 0))
    n_slab = D_REC // LANES
    gates = pl.BlockSpec((n_slab, LANES, 2 * LANES), lambda s: (0, 0, 0))
    return pl.pallas_call(
        functools.partial(_mixers_kernel, S=S, T=T),
        grid=(n_step,),
        in_specs=[head, head, head, seq, seq,
                  pl.BlockSpec((CONV_WIDTH, D_REC), lambda s: (0, 0)), vec, gates,
                  vec, vec, vec, vec],
        out_specs=[head, seq],
        out_shape=[jax.ShapeDtypeStruct((B, S, C), BF16), jax.ShapeDtypeStruct((B, S, D_REC), BF16)],
        scratch_shapes=[pltpu.VMEM((2 * BLK, 2 * BLK), BF16),
                        pltpu.VMEM((3, S, LANES), F32),
                        pltpu.VMEM((3, S, LANES), F32),
                        pltpu.VMEM((3, S, LANES), F32),
                        pltpu.VMEM((3, S, LANES), F32),
                        pltpu.VMEM((n_slab, (T + CONV_WIDTH - 1) * B, LANES), F32),
                        pltpu.VMEM((n_slab, T * B, LANES), F32),
                        pltpu.VMEM((n_slab, T * B, LANES), F32),
                        pltpu.VMEM((n_slab, B, LANES), F32)],
        compiler_params=pltpu.CompilerParams(
            dimension_semantics=("arbitrary",), vmem_limit_bytes=VMEM_LIMIT),
        name="mixers",
    )(q, k, v, xa, ga, cw, cb.reshape(1, D_REC), wg, ba.reshape(1, D_REC),
      bx.reshape(1, D_REC), lam.reshape(1, D_REC), g.reshape(1, D_REC))


def _outproj_rows(r0, x_ref, ya_ref, att_ref, gb_ref, mod_ref, natt_ref, npost_ref, w_ref,
                  o_ref, att_s):
    rows = pl.ds(r0, OUT_SUB_ROWS)
    n_slab = D_ATT // LANES
    for r in range(4):
        part = att_ref[0, r, r0 // 4:(r0 + OUT_SUB_ROWS) // 4, :].astype(F32)
        for j in range(n_slab):
            att_s[j, pl.ds(r0 + r, OUT_SUB_ROWS // 4, stride=4), :] = part[:, j * LANES:(j + 1) * LANES]
    att = jnp.concatenate([att_s[j, rows, :] for j in range(n_slab)], axis=1)
    ya = ya_ref[0, rows, :]
    yb = _rms(att * _silu(gb_ref[0, rows, :].astype(F32)), natt_ref[...]).astype(BF16)
    yield
    mix = jnp.dot(jnp.concatenate([ya, yb], axis=1), w_ref[...], preferred_element_type=F32)
    yield
    gate = mod_ref[pl.ds(pl.program_id(0), 1), 2 * D_MODEL:3 * D_MODEL]
    o_ref[0, rows, :] = x_ref[0, rows, :] + gate * _rms(mix, npost_ref[...])


def _outproj_kernel(x_ref, *refs):
    ts = x_ref.shape[1]
    _skewed([_outproj_rows(r0, x_ref, *refs) for r0 in range(0, ts, OUT_SUB_ROWS)], 3)


def _outproj_call(x, ya, att, gb, mod, natt, npost, w, ts):
    B, S, D = x.shape
    seq = lambda b, i: (b, i, 0)
    const = lambda b, i: (0, 0)
    half = pl.BlockSpec((1, ts, D_REC), seq)
    return pl.pallas_call(
        _outproj_kernel,
        grid=(B, S // ts),
        in_specs=[pl.BlockSpec((1, ts, D), seq), half,
                  pl.BlockSpec((1, 4, ts // 4, D_ATT), lambda b, i: (b, 0, i, 0)), half,
                  pl.BlockSpec((B, 3 * D), const),
                  pl.BlockSpec((1, D_ATT), const),
                  pl.BlockSpec((1, D), const),
                  pl.BlockSpec((D_REC + D_ATT, D), const)],
        out_specs=pl.BlockSpec((1, ts, D), seq),
        out_shape=jax.ShapeDtypeStruct((B, S, D), F32),
        scratch_shapes=[pltpu.VMEM((D_ATT // LANES, ts, LANES), F32)],
        compiler_params=pltpu.CompilerParams(
            dimension_semantics=("parallel", "parallel"), vmem_limit_bytes=VMEM_LIMIT),
        name="outproj",
    )(x, ya, att.reshape(B, 4, S // 4, D_ATT), gb, mod, natt.reshape(1, D_ATT), npost.reshape(1, D), w)


def _gate_weights(w_a, w_x):
    nb, n, _ = w_a.shape
    per = LANES // n
    eye = jnp.eye(per, dtype=w_a.dtype)

    def lane_blocks(w):
        w = w.reshape(nb // per, per, n, n)
        return jnp.einsum('jhik,hg->jhigk', w, eye).reshape(nb // per, LANES, LANES)

    return (0.5 * jnp.concatenate([lane_blocks(w_a), lane_blocks(w_x)], axis=-1)).astype(BF16)


def kernel(x, c, positions, w_ada, b_ada, norm_pre, norm_post, w_in, conv_w, conv_b, w_rg_a, b_rg_a,
           w_rg_x, b_rg_x, lru_lambda, norm_rec, norm_att, w_out):
    depth = w_in.shape[0]
    inv_freq = ROPE_THETA ** (-jnp.arange(HALF, dtype=F32) / HALF)
    invf = jnp.tile(inv_freq, LANES // HALF).reshape(1, LANES)
    for l in range(depth):
        mod = _mod_call(c, w_ada[l], b_ada[l])
        xa, ga, q, k, v, gb = _inproj_call(x, mod, norm_pre[l], positions, invf,
                                           w_in[l], ts=1024)
        att, ya = _mixers_call(q, k, v, xa, ga, conv_w[l], conv_b[l],
                               _gate_weights(w_rg_a[l], w_rg_x[l]),
                               b_rg_a[l], b_rg_x[l], lru_lambda[l], norm_rec[l])
        x = _outproj_call(x, ya, att, gb, mod, norm_att[l], norm_post[l],
                          w_out[l].astype(BF16), ts=1024)
    return x
```

```python
import functools

import jax
import jax.numpy as jnp
from jax import lax
from jax.experimental import pallas as pl
from jax.experimental.pallas import tpu as pltpu

F32 = jnp.float32
BF16 = jnp.bfloat16

D_MODEL = 1024
D_REC = 512
D_ATT = 512
N_LRU_BLOCKS = 8
LRU_C = 8.0
CONV_WIDTH = 4
HEAD_DIM = 64
HALF = HEAD_DIM // 2
ROPE_THETA = 10000.0
NORM_EPS = 1e-6
NEG_INF = -1e30
D_IN_PROJ = 2 * D_REC + 4 * D_ATT

LANES = 128
SUBLANES = 8
BLK = 128
DILATIONS = (1, 4, 16)
ROW_TILE = 1024
MOD_COLS = 1024
COMBINE_ROWS = 256
REC_CHUNK_STEPS = 16
SKEW = 1

VMEM_LIMIT = 56 * 1024 * 1024
Q_SCALE = HEAD_DIM ** -0.5 * 1.4426950408889634


def _sigmoid(x):
    return 0.5 * jnp.tanh(0.5 * x) + 0.5


def _silu(x):
    h = 0.5 * x
    return h + h * jnp.tanh(h)


def _rms(x, g):
    return x * lax.rsqrt(jnp.mean(x * x, axis=-1, keepdims=True) + NORM_EPS) * g


def _mod_kernel(c_ref, w_ref, b_ref, o_ref):
    o_ref[...] = jnp.dot(_silu(c_ref[...]).astype(BF16), w_ref[...].astype(BF16),
                         preferred_element_type=F32) + b_ref[...]


def _mod_call(c, w, b):
    B, D = c.shape
    N = w.shape[1]
    tn = MOD_COLS
    return pl.pallas_call(
        _mod_kernel,
        grid=(N // tn,),
        in_specs=[pl.BlockSpec((B, D), lambda j: (0, 0)),
                  pl.BlockSpec((D, tn), lambda j: (0, j)),
                  pl.BlockSpec((1, tn), lambda j: (0, j))],
        out_specs=pl.BlockSpec((B, tn), lambda j: (0, j)),
        out_shape=jax.ShapeDtypeStruct((B, N), F32),
        compiler_params=pltpu.CompilerParams(vmem_limit_bytes=VMEM_LIMIT),
        name="mod",
    )(c, w, b.reshape(1, N))


def _skewed(stages, n_stage):
    for k in range(len(stages) + n_stage - 1):
        for stage in range(n_stage):
            if 0 <= k - stage < len(stages):
                next(stages[k - stage], None)


SUB_ROWS = 256
OUT_SUB_ROWS = 512


def _inproj_chunks(r0, x_ref, mod_ref, g_ref, pos_ref, invf_ref, w_ref, *out_refs):
    rows = pl.ds(r0, SUB_ROWS)
    cache = {}

    def normed():
        if "h" not in cache:
            batch = pl.ds(pl.program_id(0), 1)
            shift = mod_ref[batch, 0:D_MODEL]
            scale = mod_ref[batch, D_MODEL:2 * D_MODEL]
            cache["h"] = (_rms(x_ref[0, rows, :], g_ref[...]) * (1.0 + scale) + shift).astype(BF16)
        return cache["h"]

    def rotary():
        if "rot" not in cache:
            n_grp = LANES // HALF
            blk = SUB_ROWS // n_grp
            pos = pos_ref[pl.ds(pl.program_id(0), 1), rows].astype(F32)
            pos = jnp.concatenate([jnp.broadcast_to(pos[:, c:c + LANES], (LANES, LANES)).T
                                   for c in range(0, SUB_ROWS, LANES)], axis=0)
            lane = lax.broadcasted_iota(jnp.int32, (1, LANES), 1)
            grp = lane // HALF
            packed = pos[0:blk]
            for a in range(1, n_grp):
                packed = jnp.where(grp == a, pos[a * blk:(a + 1) * blk], packed)
            ang = packed * invf_ref[...]

            def spread(t):
                rolled = [t] + [pltpu.roll(t, HALF * k, 1) for k in range(1, n_grp)]
                blocks = []
                for a in range(n_grp):
                    out = rolled[-a % n_grp]
                    for b in range(1, n_grp):
                        out = jnp.where(grp == b, rolled[(b - a) % n_grp], out)
                    blocks.append(out)
                return jnp.concatenate(blocks, axis=0)

            first = (lane % HEAD_DIM) < HALF
            sin = spread(jnp.sin(ang))
            cache["rot"] = (spread(jnp.cos(ang)), jnp.where(first, -sin, sin), first)
        return cache["rot"]

    def rope(t, scale):
        cos, sin_signed, first = rotary()
        cols = []
        for j in range(D_ATT // LANES):
            tj = t[:, j * LANES:(j + 1) * LANES]
            partner = jnp.where(first, pltpu.roll(tj, LANES - HALF, 1), pltpu.roll(tj, HALF, 1))
            cols.append(tj * cos + partner * sin_signed)
        out = jnp.concatenate(cols, axis=1)
        return out if scale is None else out * scale

    epilogues = [None, None, lambda t: rope(t, Q_SCALE), lambda t: rope(t, None), None, None]

    def item(c, o_ref, epilogue):
        t = jnp.dot(normed(), w_ref[:, c * D_REC:(c + 1) * D_REC], preferred_element_type=F32)
        yield
        o_ref[0, rows, :] = (t if epilogue is None else epilogue(t)).astype(o_ref.dtype)

    return [item(c, o_ref, ep) for c, (o_ref, ep) in enumerate(zip(out_refs, epilogues))]


def _inproj_kernel(x_ref, mod_ref, g_ref, pos_ref, invf_ref, w_ref, *refs):
    *out_refs, w_bf = refs
    ts = x_ref.shape[1]

    @pl.when((pl.program_id(0) == 0) & (pl.program_id(1) == 0))
    def _():
        for c in range(0, D_IN_PROJ, D_REC):
            w_bf[:, c:c + D_REC] = w_ref[:, c:c + D_REC].astype(BF16)

    items = [it for r0 in range(0, ts, SUB_ROWS)
             for it in _inproj_chunks(r0, x_ref, mod_ref, g_ref, pos_ref, invf_ref, w_bf, *out_refs)]
    _skewed(items, 2)


def _inproj_call(x, mod, g, pos, invf, w, ts):
    B, S, D = x.shape
    seq = lambda b, i: (b, i, 0)
    const = lambda b, i: (0, 0)
    half = pl.BlockSpec((1, ts, D_REC), seq)
    half_shape = jax.ShapeDtypeStruct((B, S, D_REC), F32)
    return pl.pallas_call(
        _inproj_kernel,
        grid=(B, S // ts),
        in_specs=[pl.BlockSpec((1, ts, D), seq),
                  pl.BlockSpec((B, 3 * D), const),
                  pl.BlockSpec((1, D), const),
                  pl.BlockSpec((B, ts), lambda b, i: (0, i)),
                  pl.BlockSpec((1, LANES), const),
                  pl.BlockSpec((D, D_IN_PROJ), const, pipeline_mode=pl.Buffered(1))],
        out_specs=[half] * 6,
        out_shape=[half_shape] * 5 + [jax.ShapeDtypeStruct((B, S, D_ATT), BF16)],
        scratch_shapes=[pltpu.VMEM((D, D_IN_PROJ), BF16)],
        compiler_params=pltpu.CompilerParams(
            dimension_semantics=("arbitrary", "arbitrary"), vmem_limit_bytes=VMEM_LIMIT),
        name="inproj",
    )(x, mod, g.reshape(1, D), pos, invf, w)


def _rec_stages(xa_ref, ga_ref, cw_ref, cb_ref, wg_ref, ba_ref, bx_ref, lam_ref, g_ref,
                o_ref, xs, a_s, u_s, h_s, *, T):
    B = SUBLANES
    tail = (CONV_WIDTH - 1) * B
    n_slab = D_REC // LANES
    slab = lambda j: slice(j * LANES, (j + 1) * LANES)

    for b in range(B):
        for j in range(n_slab):
            xs[j, pl.ds(tail + b, T, stride=B), :] = xa_ref[b, :, slab(j)]
        yield

    z = -lam_ref[...]
    half_rate = (-0.5 * LRU_C) * (jnp.maximum(z, 0.0) + jnp.log1p(jnp.exp(-jnp.abs(z))))
    half_ba = 0.5 * ba_ref[...]
    half_bx = 0.5 * bx_ref[...]
    chunk = REC_CHUNK_STEPS * B
    for r0 in range(0, T * B, chunk):
        for j in range(n_slab):
            xc = sum(cw_ref[k:k + 1, slab(j)] * xs[j, k * B + r0:k * B + r0 + chunk, :]
                     for k in range(CONV_WIDTH)) + cb_ref[:, slab(j)]
            pre = jnp.dot(xc.astype(BF16), wg_ref[j], preferred_element_type=F32)
            ta = jnp.tanh(pre[:, 0:LANES] + half_ba[:, slab(j)])
            ig = 0.5 * jnp.tanh(pre[:, LANES:] + half_bx[:, slab(j)]) + 0.5
            log_a = ta * half_rate[:, slab(j)] + half_rate[:, slab(j)]
            a = jnp.exp(log_a)
            a_s[j, r0:r0 + chunk, :] = a
            u_s[j, r0:r0 + chunk, :] = jnp.sqrt(-jnp.tanh(log_a) * (1.0 + a * a)) * (ig * xc)
        yield
    xs[:, 0:tail, :] = xs[:, T * B:T * B + tail, :]

    h = h_s[...]
    for t in range(T):
        rows = pl.ds(t * B, B)
        h = a_s[:, rows, :] * h + u_s[:, rows, :]
        u_s[:, rows, :] = h
        if t % 8 == 7:
            yield
    h_s[...] = h

    for b in range(B):
        hb = jnp.concatenate([u_s[j, pl.ds(b, T, stride=B), :] for j in range(n_slab)], axis=-1)
        o_ref[b] = _rms(hb * _silu(ga_ref[b].astype(F32)), g_ref[...]).astype(o_ref.dtype)
        yield


def _attn_stages(q_ref, k_ref, v_ref, o_ref, bias_s, x4_s, acc_s, m_s, l_s, *, S):
    qi = lax.broadcasted_iota(jnp.int32, (2 * BLK, 2 * BLK), 0) % BLK
    ki = lax.broadcasted_iota(jnp.int32, (2 * BLK, 2 * BLK), 1)
    dist = qi + BLK - ki
    bias_s[...] = jnp.where((dist >= 0) & (dist <= BLK), 0.0, NEG_INF).astype(bias_s.dtype)
    lane = lax.broadcasted_iota(jnp.int32, (1, LANES), 1)
    head_a = lane < HEAD_DIM
    ind_a = jnp.where(head_a, 1.0, 0.0).astype(BF16)
    ind_b = jnp.where(head_a, 0.0, 1.0).astype(BF16)

    Sq = S // 4

    def reorder(a, ref, r):
        x4_s[a, r * Sq:(r + 1) * Sq, :] = ref[0, pl.ds(r, Sq, stride=4), :]

    reorders = [functools.partial(reorder, a, ref, r)
                for a, ref in enumerate((q_ref, k_ref, v_ref)) for r in range(4)]

    natural = tuple((lambda rows, ref=ref: ref[0, rows, :]) for ref in (q_ref, k_ref, v_ref))
    mod4 = tuple((lambda rows, a=a: x4_s[a, rows, :]) for a in range(3))

    def tile(p, src, q_rows, k_rows, nk):
        qt = src[0](q_rows)
        kt = src[1](k_rows).astype(BF16)
        vt = src[2](k_rows).astype(BF16)
        q2 = jnp.concatenate([jnp.where(head_a, qt, 0.0), jnp.where(head_a, 0.0, qt)], axis=0)
        s = lax.dot_general(q2.astype(BF16), kt, (((1,), (1,)), ((), ())),
                            preferred_element_type=F32)
        yield
        sb = s.astype(BF16) + bias_s[:, 2 * BLK - nk:2 * BLK]
        m = jnp.max(sb, axis=-1, keepdims=True)
        e = jnp.exp2(sb - m)
        m = m.astype(F32)
        m_s[p, q_rows, :] = jnp.where(head_a, m[0:BLK], m[BLK:])
        yield
        v_a = jnp.concatenate([vt * ind_a, jnp.broadcast_to(ind_a, (nk, LANES))], axis=1)
        v_b = jnp.concatenate([vt * ind_b, jnp.broadcast_to(ind_b, (nk, LANES))], axis=1)
        r = jnp.dot(jnp.concatenate([e[0:BLK], e[BLK:]], axis=1), jnp.concatenate([v_a, v_b], axis=0),
                    preferred_element_type=F32)
        acc_s[p, q_rows, :] = r[:, 0:LANES]
        l_s[p, q_rows, :] = r[:, LANES:]

    def band_tile(p, src, q0, first):
        q_rows = pl.ds(q0, BLK)
        if first:
            return tile(p, src, q_rows, q_rows, BLK)
        return tile(p, src, q_rows, pl.ds(q0 - BLK, 2 * BLK), 2 * BLK)

    n_blk = Sq // BLK
    assert DILATIONS == (1, 4, 16) and S == DILATIONS[-1] * BLK
    tiles = [band_tile(0, natural, n * BLK, n == 0) for n in range(S // BLK)]
    tiles += [band_tile(1, mod4, r * Sq + n * BLK, n == 0) for r in range(4) for n in range(n_blk)]
    for r in range(4):
        for e in range(4):
            rows = pl.ds(r * Sq + e, BLK, stride=4)
            tiles.append(tile(2, mod4, rows, rows, BLK))

    rows_per = COMBINE_ROWS
    chunks_per_res = Sq // rows_per

    def combine(c):
        rows = pl.ds(c * rows_per, rows_per)
        nat_rows = pl.ds(c // chunks_per_res + (c % chunks_per_res) * (4 * rows_per), rows_per, stride=4)
        m = [m_s[0, nat_rows, :], m_s[1, rows, :], m_s[2, rows, :]]
        mx = jnp.maximum(jnp.maximum(m[0], m[1]), m[2])
        w = [jnp.exp2(mp - mx) for mp in m]
        num = w[0] * acc_s[0, nat_rows, :] + w[1] * acc_s[1, rows, :] + w[2] * acc_s[2, rows, :]
        den = w[0] * l_s[0, nat_rows, :] + w[1] * l_s[1, rows, :] + w[2] * l_s[2, rows, :]
        o_ref[0, rows, :] = (num / den).astype(o_ref.dtype)

    n_stage = 3
    first_d16 = len(tiles) - 16
    ready = {}
    for c in range(S // rows_per):
        last_tile = first_d16 + 4 * (c // chunks_per_res) + 3
        ready.setdefault(last_tile + (n_stage - 1) * SKEW + 1 + c % chunks_per_res, []).append(c)

    n_step = len(tiles) + (n_stage - 1) * SKEW
    assert len(reorders) <= S // BLK - 2
    for k in range(max(n_step, max(ready) + 1)):
        if k < len(reorders):
            reorders[k]()
        for stage in range(n_stage):
            if 0 <= k - stage * SKEW < len(tiles):
                next(tiles[k - stage * SKEW], None)
        for c in ready.get(k, []):
            combine(c)
        yield


def _mixers_kernel(q_ref, k_ref, v_ref, xa_ref, ga_ref, cw_ref, cb_ref, wg_ref, ba_ref,
                   bx_ref, lam_ref, g_ref, att_ref, ya_ref,
                   bias_s, x4_s, acc_s, m_s, l_s, xs, a_s, u_s, h_s, *, S, T):
    @pl.when(pl.program_id(0) == 0)
    def _():
        xs[:, 0:(CONV_WIDTH - 1) * SUBLANES, :] = jnp.zeros(
            (xs.shape[0], (CONV_WIDTH - 1) * SUBLANES, LANES), F32)
        h_s[...] = jnp.zeros(h_s.shape, F32)

    attn = _attn_stages(q_ref, k_ref, v_ref, att_ref, bias_s, x4_s, acc_s, m_s, l_s, S=S)
    rec = _rec_stages(xa_ref, ga_ref, cw_ref, cb_ref, wg_ref, ba_ref, bx_ref, lam_ref,
                      g_ref, ya_ref, xs, a_s, u_s, h_s, T=T)
    step = 0
    for _ in attn:
        if step % 2 == 1:
            next(rec, None)
        step += 1
    for _ in rec:
        pass


def _mixers_call(q, k, v, xa, ga, cw, cb, wg, ba, bx, lam, g):
    B, S, C = q.shape
    assert B == SUBLANES and xa.shape == (B, S, D_REC)
    n_pair = C // LANES
    n_step = B * n_pair
    T = S // n_step
    head = pl.BlockSpec((1, S, LANES), lambda s: (s // n_pair, 0, s % n_pair))
    seq = pl.BlockSpec((B, T, D_REC), lambda s: (0, s, 0))
    vec = pl.BlockSpec((1, D_REC), lambda s: (0, 0))
    n_slab = D_REC // LANES
    gates = pl.BlockSpec((n_slab, LANES, 2 * LANES), lambda s: (0, 0, 0))
    return pl.pallas_call(
        functools.partial(_mixers_kernel, S=S, T=T),
        grid=(n_step,),
        in_specs=[head, head, head, seq, seq,
                  pl.BlockSpec((CONV_WIDTH, D_REC), lambda s: (0, 0)), vec, gates,
                  vec, vec, vec, vec],
        out_specs=[head, seq],
        out_shape=[jax.ShapeDtypeStruct((B, S, C), BF16), jax.ShapeDtypeStruct((B, S, D_REC), BF16)],
        scratch_shapes=[pltpu.VMEM((2 * BLK, 2 * BLK), BF16),
                        pltpu.VMEM((3, S, LANES), F32),
                        pltpu.VMEM((3, S, LANES), F32),
                        pltpu.VMEM((3, S, LANES), F32),
                        pltpu.VMEM((3, S, LANES), F32),
                        pltpu.VMEM((n_slab, (T + CONV_WIDTH - 1) * B, LANES), F32),
                        pltpu.VMEM((n_slab, T * B, LANES), F32),
                        pltpu.VMEM((n_slab, T * B, LANES), F32),
                        pltpu.VMEM((n_slab, B, LANES), F32)],
        compiler_params=pltpu.CompilerParams(
            dimension_semantics=("arbitrary",), vmem_limit_bytes=VMEM_LIMIT),
        name="mixers",
    )(q, k, v, xa, ga, cw, cb.reshape(1, D_REC), wg, ba.reshape(1, D_REC),
      bx.reshape(1, D_REC), lam.reshape(1, D_REC), g.reshape(1, D_REC))


def _outproj_rows(r0, x_ref, ya_ref, att_ref, gb_ref, mod_ref, natt_ref, npost_ref, w_ref,
                  o_ref, att_s):
    rows = pl.ds(r0, OUT_SUB_ROWS)
    n_slab = D_ATT // LANES
    for r in range(4):
        part = att_ref[0, r, r0 // 4:(r0 + OUT_SUB_ROWS) // 4, :].astype(F32)
        for j in range(n_slab):
            att_s[j, pl.ds(r0 + r, OUT_SUB_ROWS // 4, stride=4), :] = part[:, j * LANES:(j + 1) * LANES]
    att = jnp.concatenate([att_s[j, rows, :] for j in range(n_slab)], axis=1)
    ya = ya_ref[0, rows, :]
    yb = _rms(att * _silu(gb_ref[0, rows, :].astype(F32)), natt_ref[...]).astype(BF16)
    yield
    mix = jnp.dot(jnp.concatenate([ya, yb], axis=1), w_ref[...], preferred_element_type=F32)
    yield
    gate = mod_ref[pl.ds(pl.program_id(0), 1), 2 * D_MODEL:3 * D_MODEL]
    o_ref[0, rows, :] = x_ref[0, rows, :] + gate * _rms(mix, npost_ref[...])


def _outproj_kernel(x_ref, *refs):
    ts = x_ref.shape[1]
    _skewed([_outproj_rows(r0, x_ref, *refs) for r0 in range(0, ts, OUT_SUB_ROWS)], 3)


def _outproj_call(x, ya, att, gb, mod, natt, npost, w, ts):
    B, S, D = x.shape
    seq = lambda b, i: (b, i, 0)
    const = lambda b, i: (0, 0)
    half = pl.BlockSpec((1, ts, D_REC), seq)
    return pl.pallas_call(
        _outproj_kernel,
        grid=(B, S // ts),
        in_specs=[pl.BlockSpec((1, ts, D), seq), half,
                  pl.BlockSpec((1, 4, ts // 4, D_ATT), lambda b, i: (b, 0, i, 0)), half,
                  pl.BlockSpec((B, 3 * D), const),
                  pl.BlockSpec((1, D_ATT), const),
                  pl.BlockSpec((1, D), const),
                  pl.BlockSpec((D_REC + D_ATT, D), const)],
        out_specs=pl.BlockSpec((1, ts, D), seq),
        out_shape=jax.ShapeDtypeStruct((B, S, D), F32),
        scratch_shapes=[pltpu.VMEM((D_ATT // LANES, ts, LANES), F32)],
        compiler_params=pltpu.CompilerParams(
            dimension_semantics=("parallel", "parallel"), vmem_limit_bytes=VMEM_LIMIT),
        name="outproj",
    )(x, ya, att.reshape(B, 4, S // 4, D_ATT), gb, mod, natt.reshape(1, D_ATT), npost.reshape(1, D), w)


def _gate_weights(w_a, w_x):
    nb, n, _ = w_a.shape
    per = LANES // n
    eye = jnp.eye(per, dtype=w_a.dtype)

    def lane_blocks(w):
        w = w.reshape(nb // per, per, n, n)
        return jnp.einsum('jhik,hg->jhigk', w, eye).reshape(nb // per, LANES, LANES)

    return (0.5 * jnp.concatenate([lane_blocks(w_a), lane_blocks(w_x)], axis=-1)).astype(BF16)


def kernel(x, c, positions, w_ada, b_ada, norm_pre, norm_post, w_in, conv_w, conv_b, w_rg_a, b_rg_a,
           w_rg_x, b_rg_x, lru_lambda, norm_rec, norm_att, w_out):
    depth = w_in.shape[0]
    inv_freq = ROPE_THETA ** (-jnp.arange(HALF, dtype=F32) / HALF)
    invf = jnp.tile(inv_freq, LANES // HALF).reshape(1, LANES)
    for l in range(depth):
        mod = _mod_call(c, w_ada[l], b_ada[l])
        xa, ga, q, k, v, gb = _inproj_call(x, mod, norm_pre[l], positions, invf,
                                           w_in[l], ts=ROW_TILE)
        att, ya = _mixers_call(q, k, v, xa, ga, conv_w[l], conv_b[l],
                               _gate_weights(w_rg_a[l], w_rg_x[l]),
                               b_rg_a[l], b_rg_x[l], lru_lambda[l], norm_rec[l])
        x = _outproj_call(x, ya, att, gb, mod, norm_att[l], norm_post[l],
                          w_out[l].astype(BF16), ts=ROW_TILE)
    return x
```
